```python
import math
import jax, jax.numpy as jnp
from jax import lax
import numpy as np


D_MODEL = 1024
BATCH = 1
SEQ = 16384
DEPTH = 2
DEC_BATCH = 2
DEC_SEQ = 16384
PAST_LEN = 128

HEAD_DIM = 64
CONV_CH = 384
CONV_WIDTH = 3
DIFF_HEADS = 4
DIFF_QK_DIM = 32
DIFF_V_DIM = 2 * DIFF_QK_DIM
DIL_HEADS = 6
DIL_PAIRS = ((128, 1), (512, 4), (2048, 16))
DIL_HALF = 64
N_BUCKETS = 32
MAX_DISTANCE = 1024
D_FF = 2816
Q_BLOCK = 128
EPS = 1e-6
MIX_WIDTH = CONV_CH + DIFF_HEADS * DIFF_V_DIM + DIL_HEADS * HEAD_DIM
W_CONV = 3 * CONV_CH
W_DIFF_QK = DIFF_HEADS * 2 * DIFF_QK_DIM
W_DIFF_V = DIFF_HEADS * DIFF_V_DIM
W_DIL = DIL_HEADS * HEAD_DIM
IN_WIDTH = W_CONV + 2 * W_DIFF_QK + W_DIFF_V + 3 * W_DIL
SPLITS = (CONV_CH, 2 * CONV_CH, W_CONV,
          W_CONV + W_DIFF_QK, W_CONV + 2 * W_DIFF_QK, W_CONV + 2 * W_DIFF_QK + W_DIFF_V,
          W_CONV + 2 * W_DIFF_QK + W_DIFF_V + W_DIL, W_CONV + 2 * W_DIFF_QK + W_DIFF_V + 2 * W_DIL)

kernel_name = "hybrid_parallel_encoder_two_batches"


def rms_norm(x, g):
    xf = x.astype(jnp.float32)
    y = xf * lax.rsqrt(jnp.mean(xf * xf, axis=-1, keepdims=True) + EPS)
    return (y * g.astype(jnp.float32)).astype(x.dtype)


def rel_bucket(rel):
    nb = N_BUCKETS // 2
    max_exact = nb // 2
    ret = jnp.where(rel > 0, nb, 0)
    n = jnp.abs(rel)
    nf = jnp.maximum(n, 1).astype(jnp.float32)
    large = max_exact + (jnp.log(nf / max_exact) / math.log(MAX_DISTANCE / max_exact)
                         * (nb - max_exact)).astype(jnp.int32)
    large = jnp.minimum(large, nb - 1)
    return ret + jnp.where(n < max_exact, n, large)


def swiglu(h, w_gu, w_down):
    g, u = jnp.split(h @ w_gu, 2, axis=-1)
    return (jax.nn.silu(g) * u) @ w_down


def short_conv(u, w):
    c = u.shape[-1]
    return lax.conv_general_dilated(u, w[:, None, :].astype(u.dtype), window_strides=(1,),
                                    padding=((CONV_WIDTH // 2, CONV_WIDTH // 2),),
                                    dimension_numbers=('NWC', 'WIO', 'NWC'), feature_group_count=c)


def diff_attention(q, k, v, bias_tab, lam, lam_init, sub_g):
    b, s_len, h, _, dq = q.shape
    dv = v.shape[-1]
    nblk = s_len // Q_BLOCK
    qb = q.reshape(b, nblk, Q_BLOCK, h, 2, dq).swapaxes(0, 1)
    kpos = jnp.arange(s_len)

    def block(args):
        qblk, i = args
        qpos = i * Q_BLOCK + jnp.arange(Q_BLOCK)
        bias = bias_tab[rel_bucket(kpos[None, :] - qpos[:, None])].astype(jnp.float32)
        s = jnp.einsum('bqhcd,bkhcd->bchqk', qblk, k).astype(jnp.float32) \
            + jnp.transpose(bias, (2, 0, 1))[None, None]
        p = jax.nn.softmax(s, axis=-1)
        w = p[:, 0] - lam * p[:, 1]
        return jnp.einsum('bhqk,bkhd->bqhd', w.astype(v.dtype), v)

    out = lax.map(block, (qb, jnp.arange(nblk)))
    out = out.swapaxes(0, 1).reshape(b, s_len, h, dv)
    out = rms_norm(out, sub_g) * (1.0 - lam_init)
    return out.reshape(b, s_len, h * dv)


def dilated_attention(q, k, v, bias_tab):
    b, s_len, h, d = q.shape
    nblk = s_len // Q_BLOCK
    m = jnp.arange(-DIL_HALF, DIL_HALF + 1)
    offs = [m * r for (_, r) in DIL_PAIRS]
    biases = [bias_tab[rel_bucket(o)].T.astype(jnp.float32) for o in offs]
    qb = q.reshape(b, nblk, Q_BLOCK, h, d).swapaxes(0, 1)

    def block(args):
        qblk, i = args
        qpos = i * Q_BLOCK + jnp.arange(Q_BLOCK)
        outs, lses = [], []
        for off, bias in zip(offs, biases):
            idx = qpos[:, None] + off[None, :]
            valid = (idx >= 0) & (idx < s_len)
            idx = jnp.clip(idx, 0, s_len - 1)
            kg = k[:, idx]
            vg = v[:, idx]
            s = jnp.einsum('bqhd,bqjhd->bhqj', qblk, kg).astype(jnp.float32) + bias[None, :, None, :]
            s = jnp.where(valid[None, None], s, -jnp.inf)
            lse = jax.nn.logsumexp(s, axis=-1, keepdims=True)
            p = jnp.exp(s - lse)
            outs.append(jnp.einsum('bhqj,bqjhd->bqhd', p.astype(v.dtype), vg).astype(jnp.float32))
            lses.append(lse[..., 0])
        alpha = jax.nn.softmax(jnp.stack(lses), axis=0)
        alpha = jnp.swapaxes(alpha, 2, 3)[..., None]
        return jnp.sum(alpha * jnp.stack(outs), axis=0).astype(v.dtype)

    out = lax.map(block, (qb, jnp.arange(nblk)))
    return out.swapaxes(0, 1).reshape(b, s_len, h * d)


def trunk(x, p):
    b, s_len, _ = x.shape
    for l in range(DEPTH):
        lam_init = 0.8 - 0.6 * math.exp(-0.3 * l)
        x = x + 0.5 * swiglu(rms_norm(x, p['ffn1_norm'][l]), p['ffn1_w_gu'][l], p['ffn1_w_down'][l])
        h = rms_norm(x, p['mix_norm'][l])
        proj = h @ p['w_in'][l]
        (u, bg, cg, dq, dk, dv, lq, lk, lv) = jnp.split(proj, SPLITS, axis=-1)
        y_a = bg * short_conv(cg * u, p['conv_w'][l])
        dq = rms_norm(dq.reshape(b, s_len, DIFF_HEADS, 2, DIFF_QK_DIM), p['diff_q_norm'][l]) * (DIFF_QK_DIM ** -0.5)
        dk = rms_norm(dk.reshape(b, s_len, DIFF_HEADS, 2, DIFF_QK_DIM), p['diff_k_norm'][l])
        dv = dv.reshape(b, s_len, DIFF_HEADS, DIFF_V_DIM)
        lq1 = p['lambda_q1'][l].astype(jnp.float32)
        lk1 = p['lambda_k1'][l].astype(jnp.float32)
        lq2 = p['lambda_q2'][l].astype(jnp.float32)
        lk2 = p['lambda_k2'][l].astype(jnp.float32)
        lam = jnp.exp(jnp.sum(lq1 * lk1)) - jnp.exp(jnp.sum(lq2 * lk2)) + lam_init
        y_b = diff_attention(dq, dk, dv, p['rel_bias'][:, :DIFF_HEADS], lam, lam_init, p['diff_sub_norm'][l])
        lq = rms_norm(lq.reshape(b, s_len, DIL_HEADS, HEAD_DIM), p['dil_q_norm'][l]) * (HEAD_DIM ** -0.5)
        lk = rms_norm(lk.reshape(b, s_len, DIL_HEADS, HEAD_DIM), p['dil_k_norm'][l])
        lv = lv.reshape(b, s_len, DIL_HEADS, HEAD_DIM)
        y_c = dilated_attention(lq, lk, lv, p['rel_bias'][:, DIFF_HEADS:])
        x = x + jnp.concatenate([y_a, y_b.astype(x.dtype), y_c], axis=-1) @ p['w_out'][l]
        x = x + 0.5 * swiglu(rms_norm(x, p['ffn2_norm'][l]), p['ffn2_w_gu'][l], p['ffn2_w_down'][l])
        x = rms_norm(x, p['final_norm'][l])
    return x


def setup_inputs(seed: int = 0) -> dict:
    key = jax.random.key(seed)
    ks = jax.random.split(key, 24)
    f32 = jnp.float32

    def nrm(k, shape, scale):
        return jax.random.normal(k, shape, f32) * scale

    def gain(k, shape):
        return 1.0 + 0.02 * jax.random.normal(k, shape, f32)

    return {
        'x_prompt': nrm(ks[0], (BATCH, SEQ, D_MODEL), 1.0),
        'x_sample': nrm(ks[1], (DEC_BATCH, DEC_SEQ, D_MODEL), 1.0),
        'ffn1_norm': gain(ks[2], (DEPTH, D_MODEL)),
        'ffn1_w_gu': nrm(ks[3], (DEPTH, D_MODEL, 2 * D_FF), D_MODEL ** -0.5),
        'ffn1_w_down': nrm(ks[4], (DEPTH, D_FF, D_MODEL), D_FF ** -0.5),
        'mix_norm': gain(ks[5], (DEPTH, D_MODEL)),
        'w_in': nrm(ks[6], (DEPTH, D_MODEL, IN_WIDTH), D_MODEL ** -0.5),
        'conv_w': nrm(ks[7], (DEPTH, CONV_WIDTH, CONV_CH), CONV_WIDTH ** -0.5),
        'diff_q_norm': gain(ks[8], (DEPTH, DIFF_QK_DIM)),
        'diff_k_norm': gain(ks[9], (DEPTH, DIFF_QK_DIM)),
        'lambda_q1': nrm(ks[10], (DEPTH, DIFF_QK_DIM), 0.1),
        'lambda_k1': nrm(ks[11], (DEPTH, DIFF_QK_DIM), 0.1),
        'lambda_q2': nrm(ks[12], (DEPTH, DIFF_QK_DIM), 0.1),
        'lambda_k2': nrm(ks[13], (DEPTH, DIFF_QK_DIM), 0.1),
        'diff_sub_norm': gain(ks[14], (DEPTH, DIFF_V_DIM)),
        'dil_q_norm': gain(ks[15], (DEPTH, HEAD_DIM)),
        'dil_k_norm': gain(ks[16], (DEPTH, HEAD_DIM)),
        'w_out': nrm(ks[17], (DEPTH, MIX_WIDTH, D_MODEL), MIX_WIDTH ** -0.5),
        'ffn2_norm': gain(ks[18], (DEPTH, D_MODEL)),
        'ffn2_w_gu': nrm(ks[19], (DEPTH, D_MODEL, 2 * D_FF), D_MODEL ** -0.5),
        'ffn2_w_down': nrm(ks[20], (DEPTH, D_FF, D_MODEL), D_FF ** -0.5),
        'final_norm': gain(ks[21], (DEPTH, D_MODEL)),
        'rel_bias': nrm(ks[22], (N_BUCKETS, DIFF_HEADS + DIL_HEADS), 0.2),
    }


def reference(x_prompt, x_sample, ffn1_norm, ffn1_w_gu, ffn1_w_down, mix_norm, w_in, conv_w,
              diff_q_norm, diff_k_norm, lambda_q1, lambda_k1, lambda_q2, lambda_k2, diff_sub_norm,
              dil_q_norm, dil_k_norm, w_out, ffn2_norm, ffn2_w_gu, ffn2_w_down, final_norm, rel_bias):
    p = dict(ffn1_norm=ffn1_norm, ffn1_w_gu=ffn1_w_gu, ffn1_w_down=ffn1_w_down, mix_norm=mix_norm,
             w_in=w_in, conv_w=conv_w, diff_q_norm=diff_q_norm, diff_k_norm=diff_k_norm,
             lambda_q1=lambda_q1, lambda_k1=lambda_k1, lambda_q2=lambda_q2, lambda_k2=lambda_k2,
             diff_sub_norm=diff_sub_norm, dil_q_norm=dil_q_norm, dil_k_norm=dil_k_norm, w_out=w_out,
             ffn2_norm=ffn2_norm, ffn2_w_gu=ffn2_w_gu, ffn2_w_down=ffn2_w_down, final_norm=final_norm,
             rel_bias=rel_bias)
    y_prompt = trunk(x_prompt, p)
    y_sample = trunk(x_sample, p)
    return (y_prompt, y_sample)
```

```python
import functools
import math

import jax
import jax.numpy as jnp
from jax import lax
from jax.experimental import pallas as pl
from jax.experimental.pallas import tpu as pltpu

D_MODEL = 1024
D_FF = 2816
DEPTH = 2
EPS = 1e-6
CONV_CH = 384
DIFF_HEADS = 4
DIFF_QK_DIM = 32
DIL_HEADS = 6
HEAD_DIM = 64
N_HEADS = DIFF_HEADS + DIL_HEADS
N_BUCKETS = 32
BUCKET_STEPS = (15, 27, 50, 91, 166, 305, 559)
DIL_PAIRS = ((128, 1), (512, 4), (2048, 16))

LANES = 128
TILE = 512
FFN_CHUNKS = 2
DIFF_BAND = 3
DIL_BAND = 2
NEG = -1e30
VMEM_LIMIT_BYTES = 56 * 1024 * 1024

_BF16 = jnp.bfloat16
_F32 = jnp.float32


def _params(*semantics):
    return pltpu.CompilerParams(dimension_semantics=semantics, vmem_limit_bytes=VMEM_LIMIT_BYTES)


def _dot(a, b):
    return jnp.dot(a, b, preferred_element_type=_F32)


def _rms(x, g):
    ms = jnp.mean(x * x, axis=-1, keepdims=True)
    return x * lax.rsqrt(ms + EPS) * g


def _resident(shape):
    zeros = (0,) * len(shape)
    return pl.BlockSpec(shape, lambda *_: zeros, pipeline_mode=pl.Buffered(1))


def _ffn_kernel(x_ref, g_ref, wg_ref, wu_ref, wd_ref, fg_ref, o_ref, *, final_norm):
    x = x_ref[...]
    h = _rms(x, g_ref[...]).astype(_BF16)
    fc = D_FF // FFN_CHUNKS
    acc = jnp.zeros(x.shape, _F32)
    for c in range(FFN_CHUNKS):
        g = _dot(h, wg_ref[:, c * fc:(c + 1) * fc])
        u = _dot(h, wu_ref[:, c * fc:(c + 1) * fc])
        a = (g * jax.nn.sigmoid(g) * u).astype(_BF16)
        acc = acc + _dot(a, wd_ref[c * fc:(c + 1) * fc, :])
    y = x + 0.5 * acc
    if final_norm:
        y = _rms(y, fg_ref[...])
    o_ref[...] = y


def _ffn(x2d, g, wg, wu, wd, fg, *, final_norm):
    t = x2d.shape[0]
    row = pl.BlockSpec((TILE, D_MODEL), lambda i: (i, 0))
    return pl.pallas_call(
        functools.partial(_ffn_kernel, final_norm=final_norm),
        grid=(t // TILE,),
        in_specs=[row, _resident((1, D_MODEL)), _resident((D_MODEL, D_FF)), _resident((D_MODEL, D_FF)),
                  _resident((D_FF, D_MODEL)), _resident((1, D_MODEL))],
        out_specs=row,
        out_shape=jax.ShapeDtypeStruct((t, D_MODEL), _F32),
        compiler_params=_params("parallel"),
        name="ffn_final" if final_norm else "ffn",
    )(x2d, g, wg, wu, wd, fg)


def _mixin_kernel(x_ref, xp_ref, xn_ref, g_ref, wc_ref, wq_ref, wk_ref, wv_ref, gq_ref, gk_ref, gm_ref, cw_ref,
                  ya_ref, q_ref, kt_ref, v_ref, *, tiles_per_seq):
    t = pl.program_id(0) % tiles_per_seq
    g = g_ref[...]
    h = _rms(x_ref[...], g).astype(_BF16)

    wc = wc_ref[...]
    pc = _dot(h, wc)
    cu = pc[:, 2 * CONV_CH:] * pc[:, :CONV_CH]
    bg = pc[:, CONV_CH:2 * CONV_CH]

    def halo_row(ref, r, edge):
        ph = _dot(_rms(ref[...], g).astype(_BF16), wc)
        cu_h = ph[:, 2 * CONV_CH:] * ph[:, :CONV_CH]
        return jnp.where(edge, 0.0, cu_h[r:r + 1, :])

    prev_row = halo_row(xp_ref, xp_ref.shape[0] - 1, t == 0)
    next_row = halo_row(xn_ref, 0, t == tiles_per_seq - 1)
    row = lax.broadcasted_iota(jnp.int32, cu.shape, 0)
    cu_m1 = jnp.where(row == 0, prev_row, pltpu.roll(cu, 1, 0))
    cu_p1 = jnp.where(row == TILE - 1, next_row, pltpu.roll(cu, TILE - 1, 0))
    cw = cw_ref[...]
    ya = bg * (cw[0:1, :] * cu_m1 + cw[1:2, :] * cu + cw[2:3, :] * cu_p1)
    ya_ref[...] = ya.astype(_BF16)

    def head_norm(blk, hd, gain_ref):
        gm = gm_ref[0] if hd < DIFF_HEADS else gm_ref[1]
        ms = _dot((blk * blk).astype(_BF16), gm)
        return blk * lax.rsqrt(ms + EPS) * gain_ref[:, hd * LANES:(hd + 1) * LANES]

    qf = _dot(h, wq_ref[...])
    for hd in range(N_HEADS):
        q_ref[0, hd] = head_norm(qf[:, hd * LANES:(hd + 1) * LANES], hd, gq_ref).astype(_BF16)
    kf = _dot(h, wk_ref[...])
    for hd in range(N_HEADS):
        kn = head_norm(kf[:, hd * LANES:(hd + 1) * LANES], hd, gk_ref)
        kt_ref[0, hd, 0] = kn.T.astype(_BF16)
    vf = _dot(h, wv_ref[...])
    lane = lax.broadcasted_iota(jnp.int32, (TILE, LANES), 1)
    for hd in range(N_HEADS):
        vh = jnp.where(lane == HEAD_DIM, 1.0, vf[:, hd * LANES:(hd + 1) * LANES])
        v_ref[0, hd] = vh.astype(_BF16)


def _mixin(x2d, g, wc, wq, wk, wv, gq, gk, gm, cw, *, batch, seq):
    tps = seq // TILE
    halo = 8
    hpt = TILE // halo
    n_halo = x2d.shape[0] // halo
    pw = N_HEADS * LANES
    row = pl.BlockSpec((TILE, D_MODEL), lambda i: (i, 0))
    prev = pl.BlockSpec((halo, D_MODEL), lambda i: (jnp.maximum(i * hpt - 1, 0), 0))
    nxt = pl.BlockSpec((halo, D_MODEL), lambda i: (jnp.minimum((i + 1) * hpt, n_halo - 1), 0))
    head_rows = pl.BlockSpec((1, N_HEADS, TILE, LANES), lambda i: (i // tps, 0, i % tps, 0))
    return pl.pallas_call(
        functools.partial(_mixin_kernel, tiles_per_seq=tps),
        grid=(batch * tps,),
        in_specs=[row, prev, nxt, _resident((1, D_MODEL)), _resident((D_MODEL, 3 * CONV_CH)),
                  _resident((D_MODEL, pw)), _resident((D_MODEL, pw)), _resident((D_MODEL, pw)),
                  _resident((1, pw)), _resident((1, pw)), _resident((2, LANES, LANES)),
                  _resident((3, CONV_CH))],
        out_specs=[pl.BlockSpec((TILE, CONV_CH), lambda i: (i, 0)),
                   head_rows,
                   pl.BlockSpec((1, N_HEADS, 1, LANES, TILE), lambda i: (i // tps, 0, i % tps, 0, 0)),
                   head_rows],
        out_shape=[jax.ShapeDtypeStruct((batch * seq, CONV_CH), _BF16),
                   jax.ShapeDtypeStruct((batch, N_HEADS, seq, LANES), _BF16),
                   jax.ShapeDtypeStruct((batch, N_HEADS, tps, LANES, TILE), _BF16),
                   jax.ShapeDtypeStruct((batch, N_HEADS, seq, LANES), _BF16)],
        compiler_params=_params("parallel"),
        name="mixer_in",
    )(x2d, x2d, x2d, g, wc, wq, wk, wv, gq, gk, gm, cw)


def _strip_kernel(tab_ref, o_ref, *, band, head_off, dilated):
    hd = pl.program_id(0) + head_off
    off = pl.program_id(1) - band
    ii = lax.broadcasted_iota(jnp.int32, (TILE, TILE), 0)
    jj = lax.broadcasted_iota(jnp.int32, (TILE, TILE), 1)
    rel = off * TILE + jj - ii
    n = jnp.abs(rel)
    large = jnp.full(rel.shape, N_BUCKETS // 4, jnp.int32)
    for step in BUCKET_STEPS:
        large = large + jnp.where(n >= step, 1, 0)
    bucket = jnp.where(rel > 0, N_BUCKETS // 2, 0) + jnp.where(n < N_BUCKETS // 4, n, large)
    val = jnp.zeros(rel.shape, _F32)
    for b in range(N_BUCKETS):
        val = jnp.where(bucket == b, tab_ref[b, hd], val)
    if dilated:
        cnt = jnp.zeros(rel.shape, _F32)
        for window, dil in DIL_PAIRS:
            on_grid = jnp.where((n & (dil - 1)) == 0, 1.0, 0.0)
            cnt = cnt + jnp.where(n <= window // 2, on_grid, 0.0)
        val = jnp.where(cnt == 0.0, NEG, val + jnp.log(jnp.maximum(cnt, 1.0)))
    o_ref[0, 0] = val


def _bias_strips(rel_bias, *, n_heads, band, head_off, dilated):
    n_off = 2 * band + 1
    return pl.pallas_call(
        functools.partial(_strip_kernel, band=band, head_off=head_off, dilated=dilated),
        grid=(n_heads, n_off),
        in_specs=[pl.BlockSpec(memory_space=pltpu.SMEM)],
        out_specs=pl.BlockSpec((1, 1, TILE, TILE), lambda h, o: (h, o, 0, 0)),
        out_shape=jax.ShapeDtypeStruct((n_heads, n_off, TILE, TILE), _F32),
        compiler_params=_params("parallel", "parallel"),
        name="bias_strips_dil" if dilated else "bias_strips_diff",
    )(rel_bias)


def _attn_kernel(q_ref, kt_ref, v_ref, strip_ref, gsub_ref, lamv_ref, o_ref, m_ref, acc_ref, *,
                 diff, band, n_tiles, lam_init):
    i = pl.program_id(2)
    q = q_ref[0, 0]
    lane = lax.broadcasted_iota(jnp.int32, (TILE, LANES), 1)
    if diff:
        zero = jnp.zeros_like(q)
        qs = [jnp.where(lane < DIFF_QK_DIM, q, zero), jnp.where(lane >= DIFF_QK_DIM, q, zero)]
        lo, hi = 0, n_tiles
    else:
        qs = [q]
        lo, hi = jnp.maximum(i - band, 0), jnp.minimum(i + band + 1, n_tiles)
    n_sub = len(qs)
    m_ref[...] = jnp.full(m_ref.shape, NEG, _F32)
    acc_ref[...] = jnp.zeros(acc_ref.shape, _F32)

    def body(j, carry):
        kt = kt_ref[0, 0, j]
        vt = v_ref[0, 0, pl.ds(pl.multiple_of(j * TILE, TILE), TILE), :]
        bias = strip_ref[0, jnp.clip(j - i, -band, band) + band]
        for c in range(n_sub):
            s = _dot(qs[c], kt) + bias
            m_old = m_ref[c]
            m_new = jnp.maximum(m_old, jnp.max(s, axis=-1, keepdims=True))
            alpha = jnp.exp(m_old - m_new)
            p = jnp.exp(s - m_new)
            acc_ref[c] = alpha * acc_ref[c] + _dot(p.astype(_BF16), vt)
            m_ref[c] = m_new
        return carry

    lax.fori_loop(lo, hi, body, 0)

    outs = []
    for c in range(n_sub):
        a = acc_ref[c]
        outs.append(a / a[:, HEAD_DIM:HEAD_DIM + 1])
    if diff:
        lamv = lamv_ref[...]
        lam = (jnp.exp(jnp.sum(lamv[0:1] * lamv[1:2], axis=-1, keepdims=True))
               - jnp.exp(jnp.sum(lamv[2:3] * lamv[3:4], axis=-1, keepdims=True)) + lam_init)
        o = outs[0] - lam * outs[1]
        o = jnp.where(lane < HEAD_DIM, o, 0.0)
        ms = jnp.sum(o * o, axis=-1, keepdims=True) * (1.0 / HEAD_DIM)
        y = o * lax.rsqrt(ms + EPS) * gsub_ref[...] * (1.0 - lam_init)
    else:
        y = jnp.where(lane < HEAD_DIM, outs[0], 0.0)
    o_ref[0, 0] = y.astype(_BF16)


def _attention(q, kt, v, strips, gsub, lamv, *, diff, lam_init):
    batch, _, seq, _ = q.shape
    n_tiles = seq // TILE
    n_heads = DIFF_HEADS if diff else DIL_HEADS
    head_off = 0 if diff else DIFF_HEADS
    band = DIFF_BAND if diff else DIL_BAND
    n_off = 2 * band + 1
    n_sub = 2 if diff else 1
    return pl.pallas_call(
        functools.partial(_attn_kernel, diff=diff, band=band, n_tiles=n_tiles, lam_init=lam_init),
        grid=(batch, n_heads, n_tiles),
        in_specs=[pl.BlockSpec((1, 1, TILE, LANES), lambda b, h, i: (b, h + head_off, i, 0)),
                  pl.BlockSpec((1, 1, n_tiles, LANES, TILE), lambda b, h, i: (b, h + head_off, 0, 0, 0)),
                  pl.BlockSpec((1, 1, seq, LANES), lambda b, h, i: (b, h + head_off, 0, 0)),
                  pl.BlockSpec((1, n_off, TILE, TILE), lambda b, h, i: (h, 0, 0, 0)),
                  pl.BlockSpec((1, LANES), lambda b, h, i: (0, 0)),
                  pl.BlockSpec((4, DIFF_QK_DIM), lambda b, h, i: (0, 0))],
        out_specs=pl.BlockSpec((1, 1, TILE, LANES), lambda b, h, i: (b, h, i, 0)),
        out_shape=jax.ShapeDtypeStruct((batch, n_heads, seq, LANES), _BF16),
        scratch_shapes=[pltpu.VMEM((n_sub, TILE, 1), _F32), pltpu.VMEM((n_sub, TILE, LANES), _F32)],
        compiler_params=_params("parallel", "parallel", "arbitrary"),
        name="diff_attention" if diff else "dilated_attention",
    )(q, kt, v, strips, gsub, lamv)


def _mixout_kernel(x_ref, ya_ref, yb_ref, yc_ref, wa_ref, wh_ref, o_ref):
    acc = _dot(ya_ref[...], wa_ref[...])
    for hd in range(DIFF_HEADS):
        acc = acc + _dot(yb_ref[0, hd], wh_ref[hd])
    for hd in range(DIL_HEADS):
        acc = acc + _dot(yc_ref[0, hd], wh_ref[DIFF_HEADS + hd])
    o_ref[...] = x_ref[...] + acc


def _mixout(x2d, ya, yb, yc, wa, wh, *, batch, seq):
    tps = seq // TILE
    row = pl.BlockSpec((TILE, D_MODEL), lambda i: (i, 0))
    return pl.pallas_call(
        _mixout_kernel,
        grid=(batch * tps,),
        in_specs=[row, pl.BlockSpec((TILE, CONV_CH), lambda i: (i, 0)),
                  pl.BlockSpec((1, DIFF_HEADS, TILE, LANES), lambda i: (i // tps, 0, i % tps, 0)),
                  pl.BlockSpec((1, DIL_HEADS, TILE, LANES), lambda i: (i // tps, 0, i % tps, 0)),
                  _resident((CONV_CH, D_MODEL)), _resident((N_HEADS, LANES, D_MODEL))],
        out_specs=row,
        out_shape=jax.ShapeDtypeStruct(x2d.shape, _F32),
        compiler_params=_params("parallel"),
        name="mixer_out",
    )(x2d, ya, yb, yc, wa, wh)


def _pad_heads(w, n_heads):
    w = w.reshape(w.shape[0], n_heads, HEAD_DIM)
    w = jnp.pad(w, ((0, 0), (0, 0), (0, LANES - HEAD_DIM)))
    return w.reshape(w.shape[0], n_heads * LANES)


def _head_gain(g_diff, g_dil, diff_scale, dil_scale):
    zeros = jnp.zeros((LANES - HEAD_DIM,), _F32)
    diff = jnp.concatenate([g_diff, g_diff, zeros]) * diff_scale
    dil = jnp.concatenate([g_dil, zeros]) * dil_scale
    return jnp.concatenate([jnp.tile(diff, DIFF_HEADS), jnp.tile(dil, DIL_HEADS)])[None, :]


def _group_mean_mats():
    r = jnp.arange(LANES)
    live = (r[:, None] < HEAD_DIM) & (r[None, :] < HEAD_DIM)
    same32 = (r[:, None] // DIFF_QK_DIM) == (r[None, :] // DIFF_QK_DIM)
    g_diff = jnp.where(live & same32, 1.0 / DIFF_QK_DIM, 0.0)
    g_dil = jnp.where(live, 1.0 / HEAD_DIM, 0.0)
    return jnp.stack([g_diff, g_dil]).astype(_BF16)


def kernel(x_prompt, x_sample, ffn1_norm, ffn1_w_gu, ffn1_w_down, mix_norm, w_in, conv_w, diff_q_norm, diff_k_norm,
           lambda_q1, lambda_k1, lambda_q2, lambda_k2, diff_sub_norm, dil_q_norm, dil_k_norm, w_out, ffn2_norm,
           ffn2_w_gu, ffn2_w_down, final_norm, rel_bias):
    n_prompt = x_prompt.shape[0]
    x = jnp.concatenate([x_prompt, x_sample], axis=0)
    batch, seq, _ = x.shape
    x = x.reshape(batch * seq, D_MODEL)

    strips_diff = _bias_strips(rel_bias, n_heads=DIFF_HEADS, band=DIFF_BAND, head_off=0, dilated=False)
    strips_dil = _bias_strips(rel_bias, n_heads=DIL_HEADS, band=DIL_BAND, head_off=DIFF_HEADS, dilated=True)
    gm = _group_mean_mats()
    zeros_gain = jnp.zeros((1, LANES), _F32)
    c0 = 3 * CONV_CH
    c1 = c0 + 3 * DIFF_HEADS * HEAD_DIM

    for l in range(DEPTH):
        lam_init = 0.8 - 0.6 * math.exp(-0.3 * l)

        def ffn(xin, norm, w_gu, w_down, final):
            return _ffn(xin, norm[l][None, :], w_gu[l][:, :D_FF].astype(_BF16), w_gu[l][:, D_FF:].astype(_BF16),
                        w_down[l].astype(_BF16), final_norm[l][None, :], final_norm=final)

        x = ffn(x, ffn1_norm, ffn1_w_gu, ffn1_w_down, False)

        wl = w_in[l]
        dw = DIFF_HEADS * HEAD_DIM
        lw = DIL_HEADS * HEAD_DIM
        wq = jnp.concatenate([_pad_heads(wl[:, c0:c0 + dw], DIFF_HEADS),
                              _pad_heads(wl[:, c1:c1 + lw], DIL_HEADS)], axis=1).astype(_BF16)
        wk = jnp.concatenate([_pad_heads(wl[:, c0 + dw:c0 + 2 * dw], DIFF_HEADS),
                              _pad_heads(wl[:, c1 + lw:c1 + 2 * lw], DIL_HEADS)], axis=1).astype(_BF16)
        wv = jnp.concatenate([_pad_heads(wl[:, c0 + 2 * dw:c0 + 3 * dw], DIFF_HEADS),
                              _pad_heads(wl[:, c1 + 2 * lw:c1 + 3 * lw], DIL_HEADS)], axis=1).astype(_BF16)
        gq = _head_gain(diff_q_norm[l], dil_q_norm[l], DIFF_QK_DIM ** -0.5, HEAD_DIM ** -0.5)
        gk = _head_gain(diff_k_norm[l], dil_k_norm[l], 1.0, 1.0)
        ya, q, kt, v = _mixin(x, mix_norm[l][None, :], wl[:, :c0].astype(_BF16), wq, wk, wv, gq, gk, gm, conv_w[l],
                              batch=batch, seq=seq)

        gsub = jnp.concatenate([diff_sub_norm[l], jnp.zeros((LANES - HEAD_DIM,), _F32)])[None, :]
        lamv = jnp.stack([lambda_q1[l], lambda_k1[l], lambda_q2[l], lambda_k2[l]]).astype(_F32)
        yb = _attention(q, kt, v, strips_diff, gsub, lamv, diff=True, lam_init=lam_init)
        yc = _attention(q, kt, v, strips_dil, zeros_gain, lamv, diff=False, lam_init=lam_init)

        wo = w_out[l]
        wh = wo[CONV_CH:].reshape(N_HEADS, HEAD_DIM, D_MODEL)
        wh = jnp.pad(wh, ((0, 0), (0, LANES - HEAD_DIM), (0, 0))).astype(_BF16)
        x = _mixout(x, ya, yb, yc, wo[:CONV_CH].astype(_BF16), wh, batch=batch, seq=seq)

        x = ffn(x, ffn2_norm, ffn2_w_gu, ffn2_w_down, True)

    x = x.reshape(batch, seq, D_MODEL)
    return (x[:n_prompt], x[n_prompt:])
```

```python
import functools
import math

import jax
import jax.numpy as jnp
from jax import lax
from jax.experimental import pallas as pl
from jax.experimental.pallas import tpu as pltpu

D_MODEL = 1024
D_FF = 2816
DEPTH = 2
EPS = 1e-6
CONV_CH = 384
DIFF_HEADS = 4
DIFF_QK_DIM = 32
DIL_HEADS = 6
HEAD_DIM = 64
N_HEADS = DIFF_HEADS + DIL_HEADS
N_BUCKETS = 32
BUCKET_STEPS = (15, 27, 50, 91, 166, 305, 559)
DIL_PAIRS = ((128, 1), (512, 4), (2048, 16))

LANES = 128
TILE = 512
FFN_CHUNKS = 2
DIFF_BAND = 3
DIL_BAND = 2
NEG = -1e30
LOG2E = math.log2(math.e)
SAFE_LOGIT_SPAN = 60.0
NORM_SLACK = 1.01
VMEM_LIMIT_BYTES = 56 * 1024 * 1024

_BF16 = jnp.bfloat16
_F32 = jnp.float32


def _params(*semantics):
    return pltpu.CompilerParams(dimension_semantics=semantics, vmem_limit_bytes=VMEM_LIMIT_BYTES)


def _dot(a, b):
    return jnp.dot(a, b, preferred_element_type=_F32)


def _rms(x, g):
    ms = jnp.mean(x * x, axis=-1, keepdims=True)
    return x * lax.rsqrt(ms + EPS) * g


def _resident(shape):
    zeros = (0,) * len(shape)
    return pl.BlockSpec(shape, lambda *_: zeros, pipeline_mode=pl.Buffered(1))


def _ffn_kernel(x_ref, g_ref, wg_ref, wu_ref, wd_ref, fg_ref, o_ref, *, final_norm):
    x = x_ref[...]
    h = _rms(x, g_ref[...]).astype(_BF16)
    fc = D_FF // FFN_CHUNKS
    acc = jnp.zeros(x.shape, _F32)
    for c in range(FFN_CHUNKS):
        g = _dot(h, wg_ref[:, c * fc:(c + 1) * fc])
        u = _dot(h, wu_ref[:, c * fc:(c + 1) * fc])
        a = (g * jax.nn.sigmoid(g) * u).astype(_BF16)
        acc = acc + _dot(a, wd_ref[c * fc:(c + 1) * fc, :])
    y = x + 0.5 * acc
    if final_norm:
        y = _rms(y, fg_ref[...])
    o_ref[...] = y


def _ffn(x2d, g, wg, wu, wd, fg, *, final_norm):
    t = x2d.shape[0]
    row = pl.BlockSpec((TILE, D_MODEL), lambda i: (i, 0))
    return pl.pallas_call(
        functools.partial(_ffn_kernel, final_norm=final_norm),
        grid=(t // TILE,),
        in_specs=[row, _resident((1, D_MODEL)), _resident((D_MODEL, D_FF)), _resident((D_MODEL, D_FF)),
                  _resident((D_FF, D_MODEL)), _resident((1, D_MODEL))],
        out_specs=row,
        out_shape=jax.ShapeDtypeStruct((t, D_MODEL), _F32),
        compiler_params=_params("parallel"),
        name="ffn_final" if final_norm else "ffn",
    )(x2d, g, wg, wu, wd, fg)


def _mixin_kernel(x_ref, xp_ref, xn_ref, g_ref, wc_ref, wq_ref, wk_ref, wv_ref, gq_ref, gk_ref, gm_ref, cw_ref,
                  ya_ref, q_ref, kt_ref, v_ref, *, tiles_per_seq):
    t = pl.program_id(0) % tiles_per_seq
    g = g_ref[...]
    h = _rms(x_ref[...], g).astype(_BF16)

    wc = wc_ref[...]
    pc = _dot(h, wc)
    cu = pc[:, 2 * CONV_CH:] * pc[:, :CONV_CH]
    bg = pc[:, CONV_CH:2 * CONV_CH]

    def halo_row(ref, r, edge):
        ph = _dot(_rms(ref[...], g).astype(_BF16), wc)
        cu_h = ph[:, 2 * CONV_CH:] * ph[:, :CONV_CH]
        return jnp.where(edge, 0.0, cu_h[r:r + 1, :])

    prev_row = halo_row(xp_ref, xp_ref.shape[0] - 1, t == 0)
    next_row = halo_row(xn_ref, 0, t == tiles_per_seq - 1)
    row = lax.broadcasted_iota(jnp.int32, cu.shape, 0)
    cu_m1 = jnp.where(row == 0, prev_row, pltpu.roll(cu, 1, 0))
    cu_p1 = jnp.where(row == TILE - 1, next_row, pltpu.roll(cu, TILE - 1, 0))
    cw = cw_ref[...]
    ya = bg * (cw[0:1, :] * cu_m1 + cw[1:2, :] * cu + cw[2:3, :] * cu_p1)
    ya_ref[...] = ya.astype(_BF16)

    def head_norm(blk, hd, gain_ref):
        gm = gm_ref[0] if hd < DIFF_HEADS else gm_ref[1]
        ms = _dot((blk * blk).astype(_BF16), gm)
        return blk * lax.rsqrt(ms + EPS) * gain_ref[:, hd * LANES:(hd + 1) * LANES]

    qf = _dot(h, wq_ref[...])
    for hd in range(N_HEADS):
        q_ref[0, hd] = head_norm(qf[:, hd * LANES:(hd + 1) * LANES], hd, gq_ref).astype(_BF16)
    kf = _dot(h, wk_ref[...])
    for hd in range(N_HEADS):
        kn = head_norm(kf[:, hd * LANES:(hd + 1) * LANES], hd, gk_ref)
        kt_ref[0, hd, 0] = kn.T.astype(_BF16)
    vf = _dot(h, wv_ref[...])
    lane = lax.broadcasted_iota(jnp.int32, (TILE, LANES), 1)
    for hd in range(N_HEADS):
        vh = jnp.where(lane == HEAD_DIM, 1.0, vf[:, hd * LANES:(hd + 1) * LANES])
        v_ref[0, hd] = vh.astype(_BF16)


def _mixin(x2d, g, wc, wq, wk, wv, gq, gk, gm, cw, *, batch, seq):
    tps = seq // TILE
    halo = 8
    hpt = TILE // halo
    n_halo = x2d.shape[0] // halo
    pw = N_HEADS * LANES
    row = pl.BlockSpec((TILE, D_MODEL), lambda i: (i, 0))
    prev = pl.BlockSpec((halo, D_MODEL), lambda i: (jnp.maximum(i * hpt - 1, 0), 0))
    nxt = pl.BlockSpec((halo, D_MODEL), lambda i: (jnp.minimum((i + 1) * hpt, n_halo - 1), 0))
    head_rows = pl.BlockSpec((1, N_HEADS, TILE, LANES), lambda i: (i // tps, 0, i % tps, 0))
    return pl.pallas_call(
        functools.partial(_mixin_kernel, tiles_per_seq=tps),
        grid=(batch * tps,),
        in_specs=[row, prev, nxt, _resident((1, D_MODEL)), _resident((D_MODEL, 3 * CONV_CH)),
                  _resident((D_MODEL, pw)), _resident((D_MODEL, pw)), _resident((D_MODEL, pw)),
                  _resident((1, pw)), _resident((1, pw)), _resident((2, LANES, LANES)),
                  _resident((3, CONV_CH))],
        out_specs=[pl.BlockSpec((TILE, CONV_CH), lambda i: (i, 0)),
                   head_rows,
                   pl.BlockSpec((1, N_HEADS, 1, LANES, TILE), lambda i: (i // tps, 0, i % tps, 0, 0)),
                   head_rows],
        out_shape=[jax.ShapeDtypeStruct((batch * seq, CONV_CH), _BF16),
                   jax.ShapeDtypeStruct((batch, N_HEADS, seq, LANES), _BF16),
                   jax.ShapeDtypeStruct((batch, N_HEADS, tps, LANES, TILE), _BF16),
                   jax.ShapeDtypeStruct((batch, N_HEADS, seq, LANES), _BF16)],
        compiler_params=_params("parallel"),
        name="mixer_in",
    )(x2d, x2d, x2d, g, wc, wq, wk, wv, gq, gk, gm, cw)


def _strip_kernel(tab_ref, ref_ref, o_ref, *, band, head_off, dilated):
    hd = pl.program_id(0) + head_off
    off = pl.program_id(1) - band
    ii = lax.broadcasted_iota(jnp.int32, (TILE, TILE), 0)
    jj = lax.broadcasted_iota(jnp.int32, (TILE, TILE), 1)
    rel = off * TILE + jj - ii
    n = jnp.abs(rel)
    large = jnp.full(rel.shape, N_BUCKETS // 4, jnp.int32)
    for step in BUCKET_STEPS:
        large = large + jnp.where(n >= step, 1, 0)
    bucket = jnp.where(rel > 0, N_BUCKETS // 2, 0) + jnp.where(n < N_BUCKETS // 4, n, large)
    val = jnp.zeros(rel.shape, _F32)
    for b in range(N_BUCKETS):
        val = jnp.where(bucket == b, tab_ref[b, hd], val)
    if dilated:
        cnt = jnp.zeros(rel.shape, _F32)
        for window, dil in DIL_PAIRS:
            on_grid = jnp.where((n & (dil - 1)) == 0, 1.0, 0.0)
            cnt = cnt + jnp.where(n <= window // 2, on_grid, 0.0)
        val = val + jnp.log(jnp.maximum(cnt, 1.0))
    val = (val - ref_ref[0, hd]) * ref_ref[1, hd]
    if dilated:
        val = jnp.where(cnt == 0.0, NEG, val)
    o_ref[0, 0] = val


def _bias_strips(rel_bias, score_ref, *, n_heads, band, head_off, dilated):
    n_off = 2 * band + 1
    return pl.pallas_call(
        functools.partial(_strip_kernel, band=band, head_off=head_off, dilated=dilated),
        grid=(n_heads, n_off),
        in_specs=[pl.BlockSpec(memory_space=pltpu.SMEM), pl.BlockSpec(memory_space=pltpu.SMEM)],
        out_specs=pl.BlockSpec((1, 1, TILE, TILE), lambda h, o: (h, o, 0, 0)),
        out_shape=jax.ShapeDtypeStruct((n_heads, n_off, TILE, TILE), _F32),
        compiler_params=_params("parallel", "parallel"),
        name="bias_strips_dil" if dilated else "bias_strips_diff",
    )(rel_bias, score_ref)


def _attn_kernel(q_ref, kt_ref, v_ref, strip_ref, gsub_ref, lamv_ref, o_ref, m_ref, acc_ref, *,
                 diff, band, n_tiles, lam_init, static_ref):
    i = pl.program_id(2)
    q = q_ref[0, 0]
    lane = lax.broadcasted_iota(jnp.int32, (TILE, LANES), 1)
    if diff:
        zero = jnp.zeros_like(q)
        qs = [jnp.where(lane < DIFF_QK_DIM, q, zero), jnp.where(lane >= DIFF_QK_DIM, q, zero)]
        lo, hi = 0, n_tiles
    else:
        qs = [q]
        lo, hi = jnp.maximum(i - band, 0), jnp.minimum(i + band + 1, n_tiles)
    n_sub = len(qs)
    m_ref[...] = jnp.full(m_ref.shape, NEG, _F32)
    acc_ref[...] = jnp.zeros(acc_ref.shape, _F32)

    def body(j, carry):
        kt = kt_ref[0, 0, j]
        vt = v_ref[0, 0, pl.ds(pl.multiple_of(j * TILE, TILE), TILE), :]
        bias = strip_ref[0, jnp.clip(j - i, -band, band) + band]
        for c in range(n_sub):
            s = _dot(qs[c], kt) + bias
            if static_ref:
                acc_ref[c] += _dot(jnp.exp2(s).astype(_BF16), vt)
                continue
            m_old = m_ref[c]
            m_new = jnp.maximum(m_old, jnp.max(s, axis=-1, keepdims=True))
            alpha = jnp.exp(m_old - m_new)
            p = jnp.exp(s - m_new)
            acc_ref[c] = alpha * acc_ref[c] + _dot(p.astype(_BF16), vt)
            m_ref[c] = m_new
        return carry

    lax.fori_loop(lo, hi, body, 0)

    outs = []
    for c in range(n_sub):
        a = acc_ref[c]
        outs.append(a / a[:, HEAD_DIM:HEAD_DIM + 1])
    if diff:
        lamv = lamv_ref[...]
        lam = (jnp.exp(jnp.sum(lamv[0:1] * lamv[1:2], axis=-1, keepdims=True))
               - jnp.exp(jnp.sum(lamv[2:3] * lamv[3:4], axis=-1, keepdims=True)) + lam_init)
        o = outs[0] - lam * outs[1]
        o = jnp.where(lane < HEAD_DIM, o, 0.0)
        ms = jnp.sum(o * o, axis=-1, keepdims=True) * (1.0 / HEAD_DIM)
        y = o * lax.rsqrt(ms + EPS) * gsub_ref[...] * (1.0 - lam_init)
    else:
        y = jnp.where(lane < HEAD_DIM, outs[0], 0.0)
    o_ref[0, 0] = y.astype(_BF16)


def _attention(q, kt, v, strips, gsub, lamv, *, diff, lam_init, static_ref):
    batch, _, seq, _ = q.shape
    n_tiles = seq // TILE
    n_heads = DIFF_HEADS if diff else DIL_HEADS
    head_off = 0 if diff else DIFF_HEADS
    band = DIFF_BAND if diff else DIL_BAND
    n_off = 2 * band + 1
    n_sub = 2 if diff else 1
    return pl.pallas_call(
        functools.partial(_attn_kernel, diff=diff, band=band, n_tiles=n_tiles, lam_init=lam_init,
                          static_ref=static_ref),
        grid=(batch, n_heads, n_tiles),
        in_specs=[pl.BlockSpec((1, 1, TILE, LANES), lambda b, h, i: (b, h + head_off, i, 0)),
                  pl.BlockSpec((1, 1, n_tiles, LANES, TILE), lambda b, h, i: (b, h + head_off, 0, 0, 0)),
                  pl.BlockSpec((1, 1, seq, LANES), lambda b, h, i: (b, h + head_off, 0, 0)),
                  pl.BlockSpec((1, n_off, TILE, TILE), lambda b, h, i: (h, 0, 0, 0)),
                  pl.BlockSpec((1, LANES), lambda b, h, i: (0, 0)),
                  pl.BlockSpec((4, DIFF_QK_DIM), lambda b, h, i: (0, 0))],
        out_specs=pl.BlockSpec((1, 1, TILE, LANES), lambda b, h, i: (b, h, i, 0)),
        out_shape=jax.ShapeDtypeStruct((batch, n_heads, seq, LANES), _BF16),
        scratch_shapes=[pltpu.VMEM((n_sub, TILE, 1), _F32), pltpu.VMEM((n_sub, TILE, LANES), _F32)],
        compiler_params=_params("parallel", "parallel", "arbitrary"),
        name=("diff_attention" if diff else "dilated_attention") + ("" if static_ref else "_online"),
    )(q, kt, v, strips, gsub, lamv)


def _mixout_kernel(x_ref, ya_ref, yb_ref, yc_ref, wa_ref, wh_ref, o_ref):
    acc = _dot(ya_ref[...], wa_ref[...])
    for hd in range(DIFF_HEADS):
        acc = acc + _dot(yb_ref[0, hd], wh_ref[hd])
    for hd in range(DIL_HEADS):
        acc = acc + _dot(yc_ref[0, hd], wh_ref[DIFF_HEADS + hd])
    o_ref[...] = x_ref[...] + acc


def _mixout(x2d, ya, yb, yc, wa, wh, *, batch, seq):
    tps = seq // TILE
    row = pl.BlockSpec((TILE, D_MODEL), lambda i: (i, 0))
    return pl.pallas_call(
        _mixout_kernel,
        grid=(batch * tps,),
        in_specs=[row, pl.BlockSpec((TILE, CONV_CH), lambda i: (i, 0)),
                  pl.BlockSpec((1, DIFF_HEADS, TILE, LANES), lambda i: (i // tps, 0, i % tps, 0)),
                  pl.BlockSpec((1, DIL_HEADS, TILE, LANES), lambda i: (i // tps, 0, i % tps, 0)),
                  _resident((CONV_CH, D_MODEL)), _resident((N_HEADS, LANES, D_MODEL))],
        out_specs=row,
        out_shape=jax.ShapeDtypeStruct(x2d.shape, _F32),
        compiler_params=_params("parallel"),
        name="mixer_out",
    )(x2d, ya, yb, yc, wa, wh)


def _pad_heads(w, n_heads):
    w = w.reshape(w.shape[0], n_heads, HEAD_DIM)
    w = jnp.pad(w, ((0, 0), (0, 0), (0, LANES - HEAD_DIM)))
    return w.reshape(w.shape[0], n_heads * LANES)


def _head_gain(g_diff, g_dil, diff_scale, dil_scale):
    zeros = jnp.zeros((LANES - HEAD_DIM,), _F32)
    diff = jnp.concatenate([g_diff, g_diff, zeros]) * diff_scale
    dil = jnp.concatenate([g_dil, zeros]) * dil_scale
    return jnp.concatenate([jnp.tile(diff, DIFF_HEADS), jnp.tile(dil, DIL_HEADS)])[None, :]


def _group_mean_mats():
    r = jnp.arange(LANES)
    live = (r[:, None] < HEAD_DIM) & (r[None, :] < HEAD_DIM)
    same32 = (r[:, None] // DIFF_QK_DIM) == (r[None, :] // DIFF_QK_DIM)
    g_diff = jnp.where(live & same32, 1.0 / DIFF_QK_DIM, 0.0)
    g_dil = jnp.where(live, 1.0 / HEAD_DIM, 0.0)
    return jnp.stack([g_diff, g_dil]).astype(_BF16)


def _softmax_reference(diff_q_norm, diff_k_norm, dil_q_norm, dil_k_norm, rel_bias):
    def qk_bound(gq, gk, dim):
        return math.sqrt(dim) * jnp.max(jnp.abs(gq)) * jnp.max(jnp.abs(gk)) * NORM_SLACK

    qk = jnp.concatenate([jnp.full((DIFF_HEADS,), qk_bound(diff_q_norm, diff_k_norm, DIFF_QK_DIM)),
                          jnp.full((DIL_HEADS,), qk_bound(dil_q_norm, dil_k_norm, HEAD_DIM))])
    merged = jnp.concatenate([jnp.zeros((DIFF_HEADS,), _F32),
                              jnp.full((DIL_HEADS,), math.log(len(DIL_PAIRS)), _F32)])
    hi = jnp.max(rel_bias, axis=0) + merged
    lo = jnp.min(rel_bias, axis=0)
    upper = qk + hi
    ok = jnp.max(2.0 * qk + hi - lo) <= SAFE_LOGIT_SPAN
    return upper, ok


def kernel(x_prompt, x_sample, ffn1_norm, ffn1_w_gu, ffn1_w_down, mix_norm, w_in, conv_w, diff_q_norm, diff_k_norm,
           lambda_q1, lambda_k1, lambda_q2, lambda_k2, diff_sub_norm, dil_q_norm, dil_k_norm, w_out, ffn2_norm,
           ffn2_w_gu, ffn2_w_down, final_norm, rel_bias):
    n_prompt = x_prompt.shape[0]
    x = jnp.concatenate([x_prompt, x_sample], axis=0)
    batch, seq, _ = x.shape
    x = x.reshape(batch * seq, D_MODEL)

    upper, static_ok = _softmax_reference(diff_q_norm, diff_k_norm, dil_q_norm, dil_k_norm, rel_bias)
    logit_scale = jnp.where(static_ok, LOG2E, 1.0).astype(_F32)
    logit_shift = jnp.stack([jnp.where(static_ok, upper, 0.0), jnp.full((N_HEADS,), logit_scale)]).astype(_F32)
    strips_diff = _bias_strips(rel_bias, logit_shift, n_heads=DIFF_HEADS, band=DIFF_BAND, head_off=0,
                               dilated=False)
    strips_dil = _bias_strips(rel_bias, logit_shift, n_heads=DIL_HEADS, band=DIL_BAND, head_off=DIFF_HEADS,
                              dilated=True)
    gm = _group_mean_mats()
    zeros_gain = jnp.zeros((1, LANES), _F32)
    c0 = 3 * CONV_CH
    c1 = c0 + 3 * DIFF_HEADS * HEAD_DIM

    for l in range(DEPTH):
        lam_init = 0.8 - 0.6 * math.exp(-0.3 * l)

        def ffn(xin, norm, w_gu, w_down, final):
            return _ffn(xin, norm[l][None, :], w_gu[l][:, :D_FF].astype(_BF16), w_gu[l][:, D_FF:].astype(_BF16),
                        w_down[l].astype(_BF16), final_norm[l][None, :], final_norm=final)

        x = ffn(x, ffn1_norm, ffn1_w_gu, ffn1_w_down, False)

        wl = w_in[l]
        dw = DIFF_HEADS * HEAD_DIM
        lw = DIL_HEADS * HEAD_DIM
        wq = jnp.concatenate([_pad_heads(wl[:, c0:c0 + dw], DIFF_HEADS),
                              _pad_heads(wl[:, c1:c1 + lw], DIL_HEADS)], axis=1).astype(_BF16)
        wk = jnp.concatenate([_pad_heads(wl[:, c0 + dw:c0 + 2 * dw], DIFF_HEADS),
                              _pad_heads(wl[:, c1 + lw:c1 + 2 * lw], DIL_HEADS)], axis=1).astype(_BF16)
        wv = jnp.concatenate([_pad_heads(wl[:, c0 + 2 * dw:c0 + 3 * dw], DIFF_HEADS),
                              _pad_heads(wl[:, c1 + 2 * lw:c1 + 3 * lw], DIL_HEADS)], axis=1).astype(_BF16)
        gq = _head_gain(diff_q_norm[l], dil_q_norm[l], DIFF_QK_DIM ** -0.5, HEAD_DIM ** -0.5) * logit_scale
        gk = _head_gain(diff_k_norm[l], dil_k_norm[l], 1.0, 1.0)
        ya, q, kt, v = _mixin(x, mix_norm[l][None, :], wl[:, :c0].astype(_BF16), wq, wk, wv, gq, gk, gm, conv_w[l],
                              batch=batch, seq=seq)

        gsub = jnp.concatenate([diff_sub_norm[l], jnp.zeros((LANES - HEAD_DIM,), _F32)])[None, :]
        lamv = jnp.stack([lambda_q1[l], lambda_k1[l], lambda_q2[l], lambda_k2[l]]).astype(_F32)
        def attend(strips, gain, diff):
            branches = [functools.partial(_attention, diff=diff, lam_init=lam_init, static_ref=flag)
                        for flag in (True, False)]
            return lax.cond(static_ok, *branches, q, kt, v, strips, gain, lamv)

        yb = attend(strips_diff, gsub, True)
        yc = attend(strips_dil, zeros_gain, False)

        wo = w_out[l]
        wh = wo[CONV_CH:].reshape(N_HEADS, HEAD_DIM, D_MODEL)
        wh = jnp.pad(wh, ((0, 0), (0, LANES - HEAD_DIM), (0, 0))).astype(_BF16)
        x = _mixout(x, ya, yb, yc, wo[:CONV_CH].astype(_BF16), wh, batch=batch, seq=seq)

        x = ffn(x, ffn2_norm, ffn2_w_gu, ffn2_w_down, True)

    x = x.reshape(batch, seq, D_MODEL)
    return (x[:n_prompt], x[n_prompt:])
```

```python
import functools
import math

import jax
import jax.numpy as jnp
from jax import lax
from jax.experimental import pallas as pl
from jax.experimental.pallas import tpu as pltpu

D_MODEL = 1024
D_FF = 2816
DEPTH = 2
EPS = 1e-6
CONV_CH = 384
DIFF_HEADS = 4
DIFF_QK_DIM = 32
DIL_HEADS = 6
HEAD_DIM = 64
N_HEADS = DIFF_HEADS + DIL_HEADS
N_BUCKETS = 32
BUCKET_STEPS = (15, 27, 50, 91, 166, 305, 559)
DIL_PAIRS = ((128, 1), (512, 4), (2048, 16))

LANES = 128
TILE = 512
FFN_CHUNKS = 2
V_ROWS = 80
KEY_TILES_PER_STEP = 4
DIFF_BAND = 3
DIL_BAND = 2
NEG = -1e30
LOG2E = math.log2(math.e)
SAFE_LOGIT_SPAN = 60.0
NORM_SLACK = 1.01
VMEM_LIMIT_BYTES = 56 * 1024 * 1024

_BF16 = jnp.bfloat16
_F32 = jnp.float32


def _params(*semantics):
    return pltpu.CompilerParams(dimension_semantics=semantics, vmem_limit_bytes=VMEM_LIMIT_BYTES)


def _dot(a, b):
    return jnp.dot(a, b, preferred_element_type=_F32)


def _rms(x, g):
    ms = jnp.mean(x * x, axis=-1, keepdims=True)
    return x * lax.rsqrt(ms + EPS) * g


def _resident(shape):
    zeros = (0,) * len(shape)
    return pl.BlockSpec(shape, lambda *_: zeros, pipeline_mode=pl.Buffered(1))


def _ffn_kernel(x_ref, g_ref, wg_ref, wu_ref, wd_ref, fg_ref, o_ref, *, final_norm):
    x = x_ref[...]
    h = _rms(x, g_ref[...]).astype(_BF16)
    fc = D_FF // FFN_CHUNKS
    acc = jnp.zeros(x.shape, _F32)
    for c in range(FFN_CHUNKS):
        g = _dot(h, wg_ref[:, c * fc:(c + 1) * fc])
        u = _dot(h, wu_ref[:, c * fc:(c + 1) * fc])
        a = (g * jax.nn.sigmoid(g) * u).astype(_BF16)
        acc = acc + _dot(a, wd_ref[c * fc:(c + 1) * fc, :])
    y = x + 0.5 * acc
    if final_norm:
        y = _rms(y, fg_ref[...])
    o_ref[...] = y


def _ffn(x2d, g, wg, wu, wd, fg, *, final_norm):
    t = x2d.shape[0]
    row = pl.BlockSpec((TILE, D_MODEL), lambda i: (i, 0))
    return pl.pallas_call(
        functools.partial(_ffn_kernel, final_norm=final_norm),
        grid=(t // TILE,),
        in_specs=[row, _resident((1, D_MODEL)), _resident((D_MODEL, D_FF)), _resident((D_MODEL, D_FF)),
                  _resident((D_FF, D_MODEL)), _resident((1, D_MODEL))],
        out_specs=row,
        out_shape=jax.ShapeDtypeStruct((t, D_MODEL), _F32),
        compiler_params=_params("parallel"),
        name="ffn_final" if final_norm else "ffn",
    )(x2d, g, wg, wu, wd, fg)


def _mixin_kernel(x_ref, xp_ref, xn_ref, g_ref, wc_ref, wq_ref, wk_ref, wv_ref, gq_ref, gk_ref, gm_ref, cw_ref,
                  ya_ref, qt_ref, k_ref, vt_ref, *, tiles_per_seq):
    t = pl.program_id(0) % tiles_per_seq
    g = g_ref[...]
    h = _rms(x_ref[...], g).astype(_BF16)

    wc = wc_ref[...]
    pc = _dot(h, wc)
    cu = pc[:, 2 * CONV_CH:] * pc[:, :CONV_CH]
    bg = pc[:, CONV_CH:2 * CONV_CH]

    def halo_row(ref, r, edge):
        ph = _dot(_rms(ref[...], g).astype(_BF16), wc)
        cu_h = ph[:, 2 * CONV_CH:] * ph[:, :CONV_CH]
        return jnp.where(edge, 0.0, cu_h[r:r + 1, :])

    prev_row = halo_row(xp_ref, xp_ref.shape[0] - 1, t == 0)
    next_row = halo_row(xn_ref, 0, t == tiles_per_seq - 1)
    row = lax.broadcasted_iota(jnp.int32, cu.shape, 0)
    cu_m1 = jnp.where(row == 0, prev_row, pltpu.roll(cu, 1, 0))
    cu_p1 = jnp.where(row == TILE - 1, next_row, pltpu.roll(cu, TILE - 1, 0))
    cw = cw_ref[...]
    ya = bg * (cw[0:1, :] * cu_m1 + cw[1:2, :] * cu + cw[2:3, :] * cu_p1)
    ya_ref[...] = ya.astype(_BF16)

    def head_norm(blk, hd, gain_ref):
        gm = gm_ref[0] if hd < DIFF_HEADS else gm_ref[1]
        ms = _dot((blk * blk).astype(_BF16), gm)
        return blk * lax.rsqrt(ms + EPS) * gain_ref[:, hd * LANES:(hd + 1) * LANES]

    qf = _dot(h, wq_ref[...])
    for hd in range(N_HEADS):
        qn = head_norm(qf[:, hd * LANES:(hd + 1) * LANES], hd, gq_ref)
        qt_ref[0, hd, 0] = qn.T.astype(_BF16)
    kf = _dot(h, wk_ref[...])
    for hd in range(N_HEADS):
        k_ref[0, hd] = head_norm(kf[:, hd * LANES:(hd + 1) * LANES], hd, gk_ref).astype(_BF16)
    vf = _dot(h, wv_ref[...])
    lane = lax.broadcasted_iota(jnp.int32, (TILE, LANES), 1)
    for hd in range(N_HEADS):
        vh = jnp.where(lane == HEAD_DIM, 1.0, vf[:, hd * LANES:(hd + 1) * LANES])
        vt_ref[0, hd, 0] = vh.T.astype(_BF16)


def _mixin(x2d, g, wc, wq, wk, wv, gq, gk, gm, cw, *, batch, seq):
    tps = seq // TILE
    halo = 8
    hpt = TILE // halo
    n_halo = x2d.shape[0] // halo
    pw = N_HEADS * LANES
    row = pl.BlockSpec((TILE, D_MODEL), lambda i: (i, 0))
    prev = pl.BlockSpec((halo, D_MODEL), lambda i: (jnp.maximum(i * hpt - 1, 0), 0))
    nxt = pl.BlockSpec((halo, D_MODEL), lambda i: (jnp.minimum((i + 1) * hpt, n_halo - 1), 0))
    head_rows = pl.BlockSpec((1, N_HEADS, TILE, LANES), lambda i: (i // tps, 0, i % tps, 0))
    head_cols = pl.BlockSpec((1, N_HEADS, 1, LANES, TILE), lambda i: (i // tps, 0, i % tps, 0, 0))
    return pl.pallas_call(
        functools.partial(_mixin_kernel, tiles_per_seq=tps),
        grid=(batch * tps,),
        in_specs=[row, prev, nxt, _resident((1, D_MODEL)), _resident((D_MODEL, 3 * CONV_CH)),
                  _resident((D_MODEL, pw)), _resident((D_MODEL, pw)), _resident((D_MODEL, pw)),
                  _resident((1, pw)), _resident((1, pw)), _resident((2, LANES, LANES)),
                  _resident((3, CONV_CH))],
        out_specs=[pl.BlockSpec((TILE, CONV_CH), lambda i: (i, 0)), head_cols, head_rows, head_cols],
        out_shape=[jax.ShapeDtypeStruct((batch * seq, CONV_CH), _BF16),
                   jax.ShapeDtypeStruct((batch, N_HEADS, tps, LANES, TILE), _BF16),
                   jax.ShapeDtypeStruct((batch, N_HEADS, seq, LANES), _BF16),
                   jax.ShapeDtypeStruct((batch, N_HEADS, tps, LANES, TILE), _BF16)],
        compiler_params=_params("parallel"),
        name="mixer_in",
    )(x2d, x2d, x2d, g, wc, wq, wk, wv, gq, gk, gm, cw)


def _strip_kernel(tab_ref, ref_ref, o_ref, *, band, head_off, dilated):
    hd = pl.program_id(0) + head_off
    off = pl.program_id(1) - band
    key = lax.broadcasted_iota(jnp.int32, (TILE, TILE), 0)
    qry = lax.broadcasted_iota(jnp.int32, (TILE, TILE), 1)
    rel = off * TILE + key - qry
    n = jnp.abs(rel)
    large = jnp.full(rel.shape, N_BUCKETS // 4, jnp.int32)
    for step in BUCKET_STEPS:
        large = large + jnp.where(n >= step, 1, 0)
    bucket = jnp.where(rel > 0, N_BUCKETS // 2, 0) + jnp.where(n < N_BUCKETS // 4, n, large)
    val = jnp.zeros(rel.shape, _F32)
    for b in range(N_BUCKETS):
        val = jnp.where(bucket == b, tab_ref[b, hd], val)
    if dilated:
        cnt = jnp.zeros(rel.shape, _F32)
        for window, dil in DIL_PAIRS:
            on_grid = jnp.where((n & (dil - 1)) == 0, 1.0, 0.0)
            cnt = cnt + jnp.where(n <= window // 2, on_grid, 0.0)
        val = val + jnp.log(jnp.maximum(cnt, 1.0))
    val = (val - ref_ref[0, hd]) * ref_ref[1, hd]
    if dilated:
        masked = jnp.logical_or(cnt == 0.0, pl.program_id(1) == 2 * band + 1)
        val = jnp.where(masked, NEG, val)
    o_ref[0, 0] = val


def _bias_strips(rel_bias, score_ref, *, n_heads, band, head_off, dilated):
    n_off = 2 * band + (2 if dilated else 1)
    return pl.pallas_call(
        functools.partial(_strip_kernel, band=band, head_off=head_off, dilated=dilated),
        grid=(n_heads, n_off),
        in_specs=[pl.BlockSpec(memory_space=pltpu.SMEM), pl.BlockSpec(memory_space=pltpu.SMEM)],
        out_specs=pl.BlockSpec((1, 1, TILE, TILE), lambda h, o: (h, o, 0, 0)),
        out_shape=jax.ShapeDtypeStruct((n_heads, n_off, TILE, TILE), _F32),
        compiler_params=_params("parallel", "parallel"),
        name="bias_strips_dil" if dilated else "bias_strips_diff",
    )(rel_bias, score_ref)


def _attn_kernel(qt_ref, k_ref, vt_ref, strip_ref, gsub_ref, lamv_ref, o_ref, m_ref, acc_ref, s_ref, *,
                 diff, band, n_tiles, lam_init, static_ref):
    i = pl.program_id(2)
    qt = qt_ref[0, 0, 0]
    feat = lax.broadcasted_iota(jnp.int32, (LANES, TILE), 0)
    if diff:
        zero = jnp.zeros_like(qt)
        qs = [jnp.where(feat < DIFF_QK_DIM, qt, zero), jnp.where(feat >= DIFF_QK_DIM, qt, zero)]
    else:
        qs = [qt]
    n_sub = len(qs)
    m_ref[...] = jnp.full(m_ref.shape, NEG, _F32)
    acc_ref[...] = jnp.zeros(acc_ref.shape, _F32)

    def body(j, carry):
        k = k_ref[0, 0, pl.ds(pl.multiple_of(j * TILE, TILE), TILE), :]
        vt = vt_ref[0, 0, j, :V_ROWS, :]
        bias = strip_ref[0, jnp.clip(j - i, -band, band) + band]
        for c in range(n_sub):
            s = _dot(k, qs[c]) + bias
            if static_ref:
                acc_ref[c, :V_ROWS, :] += _dot(vt, jnp.exp2(s).astype(_BF16))
                continue
            m_old = m_ref[c]
            m_new = jnp.maximum(m_old, jnp.max(s, axis=0, keepdims=True))
            alpha = jnp.exp(m_old - m_new)
            p = jnp.exp(s - m_new)
            acc_ref[c, :V_ROWS, :] = alpha * acc_ref[c, :V_ROWS, :] + _dot(vt, p.astype(_BF16))
            m_ref[c] = m_new
        return carry

    def scores(j, slot):
        k = k_ref[0, 0, pl.ds(pl.multiple_of(j * TILE, TILE), TILE), :]
        for c in range(n_sub):
            s_ref[slot, c] = _dot(k, qs[c])

    def consume(j, slot, strip_index):
        vt = vt_ref[0, 0, j, :V_ROWS, :]
        bias = strip_ref[0, strip_index]
        for c in range(n_sub):
            acc_ref[c, :V_ROWS, :] += _dot(vt, jnp.exp2(s_ref[slot, c] + bias).astype(_BF16))

    def group(jj, carry):
        j0 = KEY_TILES_PER_STEP * jj
        for u in range(KEY_TILES_PER_STEP):
            j = j0 + u
            scores(jnp.minimum(j + 1, n_tiles - 1), (u + 1) % 2)
            consume(j, u % 2, jnp.clip(j - i, -band, band) + band)
        return carry

    if static_ref and diff and n_tiles % KEY_TILES_PER_STEP == 0:
        scores(0, 0)
        lax.fori_loop(0, n_tiles // KEY_TILES_PER_STEP, group, 0)
    elif static_ref and not diff:
        offsets = list(range(-band, band + 1))
        tiles = [jnp.clip(i + o, 0, n_tiles - 1) for o in offsets]
        scores(tiles[0], 0)
        for u, o in enumerate(offsets):
            if u + 1 < len(offsets):
                scores(tiles[u + 1], (u + 1) % 2)
            in_range = jnp.logical_and(i + o >= 0, i + o < n_tiles)
            consume(tiles[u], u % 2, jnp.where(in_range, o + band, 2 * band + 1))
    elif diff:
        lax.fori_loop(0, n_tiles, body, 0)
    else:
        lax.fori_loop(jnp.maximum(i - band, 0), jnp.minimum(i + band + 1, n_tiles), body, 0)

    outs = []
    for c in range(n_sub):
        a = acc_ref[c]
        outs.append(a / a[HEAD_DIM:HEAD_DIM + 1, :])
    if diff:
        lamv = lamv_ref[...]
        lam = (jnp.exp(jnp.sum(lamv[0:1] * lamv[1:2], axis=-1, keepdims=True))
               - jnp.exp(jnp.sum(lamv[2:3] * lamv[3:4], axis=-1, keepdims=True)) + lam_init)
        o = outs[0] - lam * outs[1]
        o = jnp.where(feat < HEAD_DIM, o, 0.0)
        ms = jnp.sum(o * o, axis=0, keepdims=True) * (1.0 / HEAD_DIM)
        y = o * lax.rsqrt(ms + EPS) * gsub_ref[...] * (1.0 - lam_init)
    else:
        y = jnp.where(feat < HEAD_DIM, outs[0], 0.0)
    o_ref[0, 0] = y.T.astype(_BF16)


def _attention(qt, k, vt, strips, gsub, lamv, *, diff, lam_init, static_ref):
    batch, _, seq, _ = k.shape
    n_tiles = seq // TILE
    n_heads = DIFF_HEADS if diff else DIL_HEADS
    head_off = 0 if diff else DIFF_HEADS
    band = DIFF_BAND if diff else DIL_BAND
    n_off = strips.shape[1]
    n_sub = 2 if diff else 1
    return pl.pallas_call(
        functools.partial(_attn_kernel, diff=diff, band=band, n_tiles=n_tiles, lam_init=lam_init,
                          static_ref=static_ref),
        grid=(batch, n_heads, n_tiles),
        in_specs=[pl.BlockSpec((1, 1, 1, LANES, TILE), lambda b, h, i: (b, h + head_off, i, 0, 0)),
                  pl.BlockSpec((1, 1, seq, LANES), lambda b, h, i: (b, h + head_off, 0, 0)),
                  pl.BlockSpec((1, 1, n_tiles, LANES, TILE), lambda b, h, i: (b, h + head_off, 0, 0, 0)),
                  pl.BlockSpec((1, n_off, TILE, TILE), lambda b, h, i: (h, 0, 0, 0)),
                  pl.BlockSpec((LANES, 1), lambda b, h, i: (0, 0)),
                  pl.BlockSpec((4, DIFF_QK_DIM), lambda b, h, i: (0, 0))],
        out_specs=pl.BlockSpec((1, 1, TILE, LANES), lambda b, h, i: (b, h, i, 0)),
        out_shape=jax.ShapeDtypeStruct((batch, n_heads, seq, LANES), _BF16),
        scratch_shapes=[pltpu.VMEM((n_sub, 1, TILE), _F32), pltpu.VMEM((n_sub, LANES, TILE), _F32),
                        pltpu.VMEM((2, n_sub, TILE, TILE), _F32)],
        compiler_params=_params("parallel", "parallel", "arbitrary"),
        name=("diff_attention" if diff else "dilated_attention") + ("" if static_ref else "_online"),
    )(qt, k, vt, strips, gsub, lamv)


def _mixout_kernel(x_ref, ya_ref, yb_ref, yc_ref, wa_ref, wh_ref, o_ref):
    acc = _dot(ya_ref[...], wa_ref[...])
    for hd in range(DIFF_HEADS):
        acc = acc + _dot(yb_ref[0, hd], wh_ref[hd])
    for hd in range(DIL_HEADS):
        acc = acc + _dot(yc_ref[0, hd], wh_ref[DIFF_HEADS + hd])
    o_ref[...] = x_ref[...] + acc


def _mixout(x2d, ya, yb, yc, wa, wh, *, batch, seq):
    tps = seq // TILE
    row = pl.BlockSpec((TILE, D_MODEL), lambda i: (i, 0))
    return pl.pallas_call(
        _mixout_kernel,
        grid=(batch * tps,),
        in_specs=[row, pl.BlockSpec((TILE, CONV_CH), lambda i: (i, 0)),
                  pl.BlockSpec((1, DIFF_HEADS, TILE, LANES), lambda i: (i // tps, 0, i % tps, 0)),
                  pl.BlockSpec((1, DIL_HEADS, TILE, LANES), lambda i: (i // tps, 0, i % tps, 0)),
                  _resident((CONV_CH, D_MODEL)), _resident((N_HEADS, LANES, D_MODEL))],
        out_specs=row,
        out_shape=jax.ShapeDtypeStruct(x2d.shape, _F32),
        compiler_params=_params("parallel"),
        name="mixer_out",
    )(x2d, ya, yb, yc, wa, wh)


def _pad_heads(w, n_heads):
    w = w.reshape(w.shape[0], n_heads, HEAD_DIM)
    w = jnp.pad(w, ((0, 0), (0, 0), (0, LANES - HEAD_DIM)))
    return w.reshape(w.shape[0], n_heads * LANES)


def _head_gain(g_diff, g_dil, diff_scale, dil_scale):
    zeros = jnp.zeros((LANES - HEAD_DIM,), _F32)
    diff = jnp.concatenate([g_diff, g_diff, zeros]) * diff_scale
    dil = jnp.concatenate([g_dil, zeros]) * dil_scale
    return jnp.concatenate([jnp.tile(diff, DIFF_HEADS), jnp.tile(dil, DIL_HEADS)])[None, :]


def _group_mean_mats():
    r = jnp.arange(LANES)
    live = (r[:, None] < HEAD_DIM) & (r[None, :] < HEAD_DIM)
    same32 = (r[:, None] // DIFF_QK_DIM) == (r[None, :] // DIFF_QK_DIM)
    g_diff = jnp.where(live & same32, 1.0 / DIFF_QK_DIM, 0.0)
    g_dil = jnp.where(live, 1.0 / HEAD_DIM, 0.0)
    return jnp.stack([g_diff, g_dil]).astype(_BF16)


def _softmax_reference(diff_q_norm, diff_k_norm, dil_q_norm, dil_k_norm, rel_bias):
    def qk_bound(gq, gk, dim):
        return math.sqrt(dim) * jnp.max(jnp.abs(gq)) * jnp.max(jnp.abs(gk)) * NORM_SLACK

    qk = jnp.concatenate([jnp.full((DIFF_HEADS,), qk_bound(diff_q_norm, diff_k_norm, DIFF_QK_DIM)),
                          jnp.full((DIL_HEADS,), qk_bound(dil_q_norm, dil_k_norm, HEAD_DIM))])
    merged = jnp.concatenate([jnp.zeros((DIFF_HEADS,), _F32),
                              jnp.full((DIL_HEADS,), math.log(len(DIL_PAIRS)), _F32)])
    hi = jnp.max(rel_bias, axis=0) + merged
    lo = jnp.min(rel_bias, axis=0)
    upper = qk + hi
    ok = jnp.max(2.0 * qk + hi - lo) <= SAFE_LOGIT_SPAN
    return upper, ok


def kernel(x_prompt, x_sample, ffn1_norm, ffn1_w_gu, ffn1_w_down, mix_norm, w_in, conv_w, diff_q_norm, diff_k_norm,
           lambda_q1, lambda_k1, lambda_q2, lambda_k2, diff_sub_norm, dil_q_norm, dil_k_norm, w_out, ffn2_norm,
           ffn2_w_gu, ffn2_w_down, final_norm, rel_bias):
    n_prompt = x_prompt.shape[0]
    x = jnp.concatenate([x_prompt, x_sample], axis=0)
    batch, seq, _ = x.shape
    x = x.reshape(batch * seq, D_MODEL)

    upper, static_ok = _softmax_reference(diff_q_norm, diff_k_norm, dil_q_norm, dil_k_norm, rel_bias)
    logit_scale = jnp.where(static_ok, LOG2E, 1.0).astype(_F32)
    logit_shift = jnp.stack([jnp.where(static_ok, upper, 0.0), jnp.full((N_HEADS,), logit_scale)]).astype(_F32)
    strips_diff = _bias_strips(rel_bias, logit_shift, n_heads=DIFF_HEADS, band=DIFF_BAND, head_off=0,
                               dilated=False)
    strips_dil = _bias_strips(rel_bias, logit_shift, n_heads=DIL_HEADS, band=DIL_BAND, head_off=DIFF_HEADS,
                              dilated=True)
    gm = _group_mean_mats()
    zeros_gain = jnp.zeros((LANES, 1), _F32)
    c0 = 3 * CONV_CH
    c1 = c0 + 3 * DIFF_HEADS * HEAD_DIM

    for l in range(DEPTH):
        lam_init = 0.8 - 0.6 * math.exp(-0.3 * l)

        def ffn(xin, norm, w_gu, w_down, final):
            return _ffn(xin, norm[l][None, :], w_gu[l][:, :D_FF].astype(_BF16), w_gu[l][:, D_FF:].astype(_BF16),
                        w_down[l].astype(_BF16), final_norm[l][None, :], final_norm=final)

        x = ffn(x, ffn1_norm, ffn1_w_gu, ffn1_w_down, False)

        wl = w_in[l]
        dw = DIFF_HEADS * HEAD_DIM
        lw = DIL_HEADS * HEAD_DIM
        wq = jnp.concatenate([_pad_heads(wl[:, c0:c0 + dw], DIFF_HEADS),
                              _pad_heads(wl[:, c1:c1 + lw], DIL_HEADS)], axis=1).astype(_BF16)
        wk = jnp.concatenate([_pad_heads(wl[:, c0 + dw:c0 + 2 * dw], DIFF_HEADS),
                              _pad_heads(wl[:, c1 + lw:c1 + 2 * lw], DIL_HEADS)], axis=1).astype(_BF16)
        wv = jnp.concatenate([_pad_heads(wl[:, c0 + 2 * dw:c0 + 3 * dw], DIFF_HEADS),
                              _pad_heads(wl[:, c1 + 2 * lw:c1 + 3 * lw], DIL_HEADS)], axis=1).astype(_BF16)
        gq = _head_gain(diff_q_norm[l], dil_q_norm[l], DIFF_QK_DIM ** -0.5, HEAD_DIM ** -0.5) * logit_scale
        gk = _head_gain(diff_k_norm[l], dil_k_norm[l], 1.0, 1.0)
        ya, qt, k, vt = _mixin(x, mix_norm[l][None, :], wl[:, :c0].astype(_BF16), wq, wk, wv, gq, gk, gm, conv_w[l],
                              batch=batch, seq=seq)

        gsub = jnp.concatenate([diff_sub_norm[l], jnp.zeros((LANES - HEAD_DIM,), _F32)])[:, None]
        lamv = jnp.stack([lambda_q1[l], lambda_k1[l], lambda_q2[l], lambda_k2[l]]).astype(_F32)
        def attend(strips, gain, diff):
            branches = [functools.partial(_attention, diff=diff, lam_init=lam_init, static_ref=flag)
                        for flag in (True, False)]
            return lax.cond(static_ok, *branches, qt, k, vt, strips, gain, lamv)

        yb = attend(strips_diff, gsub, True)
        yc = attend(strips_dil, zeros_gain, False)

        wo = w_out[l]
        wh = wo[CONV_CH:].reshape(N_HEADS, HEAD_DIM, D_MODEL)
        wh = jnp.pad(wh, ((0, 0), (0, LANES - HEAD_DIM), (0, 0))).astype(_BF16)
        x = _mixout(x, ya, yb, yc, wo[:CONV_CH].astype(_BF16), wh, batch=batch, seq=seq)

        x = ffn(x, ffn2_norm, ffn2_w_gu, ffn2_w_down, True)

    x = x.reshape(batch, seq, D_MODEL)
    return (x[:n_prompt], x[n_prompt:])
```

```python
import functools
import math

import jax
import jax.numpy as jnp
from jax import lax
from jax.experimental import pallas as pl
from jax.experimental.pallas import tpu as pltpu

D_MODEL = 1024
D_FF = 2816
DEPTH = 2
EPS = 1e-6
CONV_CH = 384
DIFF_HEADS = 4
DIFF_QK_DIM = 32
DIL_HEADS = 6
HEAD_DIM = 64
N_HEADS = DIFF_HEADS + DIL_HEADS
N_BUCKETS = 32
BUCKET_STEPS = (15, 27, 50, 91, 166, 305, 559)
DIL_PAIRS = ((128, 1), (512, 4), (2048, 16))

LANES = 128
TILE = 512
FFN_CHUNKS = 2
V_ROWS = 80
KEY_TILES_PER_STEP = 8
DIFF_BAND = 3
DIL_BAND = 2
NEG = -1e30
LOG2E = math.log2(math.e)
SAFE_LOGIT_SPAN = 60.0
NORM_SLACK = 1.01
VMEM_LIMIT_BYTES = 56 * 1024 * 1024

_BF16 = jnp.bfloat16
_F32 = jnp.float32


def _params(*semantics):
    return pltpu.CompilerParams(dimension_semantics=semantics, vmem_limit_bytes=VMEM_LIMIT_BYTES)


def _dot(a, b):
    return jnp.dot(a, b, preferred_element_type=_F32)


def _rms(x, g):
    ms = jnp.mean(x * x, axis=-1, keepdims=True)
    return x * lax.rsqrt(ms + EPS) * g


def _resident(shape):
    zeros = (0,) * len(shape)
    return pl.BlockSpec(shape, lambda *_: zeros, pipeline_mode=pl.Buffered(1))


def _ffn_kernel(x_ref, g_ref, wg_ref, wu_ref, wd_ref, fg_ref, o_ref, *, final_norm):
    x = x_ref[...]
    h = _rms(x, g_ref[...]).astype(_BF16)
    fc = D_FF // FFN_CHUNKS
    acc = jnp.zeros(x.shape, _F32)
    for c in range(FFN_CHUNKS):
        g = _dot(h, wg_ref[:, c * fc:(c + 1) * fc])
        u = _dot(h, wu_ref[:, c * fc:(c + 1) * fc])
        a = (g * jax.nn.sigmoid(g) * u).astype(_BF16)
        acc = acc + _dot(a, wd_ref[c * fc:(c + 1) * fc, :])
    y = x + 0.5 * acc
    if final_norm:
        y = _rms(y, fg_ref[...])
    o_ref[...] = y


def _ffn(x2d, g, wg, wu, wd, fg, *, final_norm):
    t = x2d.shape[0]
    row = pl.BlockSpec((TILE, D_MODEL), lambda i: (i, 0))
    return pl.pallas_call(
        functools.partial(_ffn_kernel, final_norm=final_norm),
        grid=(t // TILE,),
        in_specs=[row, _resident((1, D_MODEL)), _resident((D_MODEL, D_FF)), _resident((D_MODEL, D_FF)),
                  _resident((D_FF, D_MODEL)), _resident((1, D_MODEL))],
        out_specs=row,
        out_shape=jax.ShapeDtypeStruct((t, D_MODEL), _F32),
        compiler_params=_params("parallel"),
        name="ffn_final" if final_norm else "ffn",
    )(x2d, g, wg, wu, wd, fg)


def _mixin_kernel(x_ref, xp_ref, xn_ref, g_ref, wc_ref, wq_ref, wk_ref, wv_ref, gq_ref, gk_ref, gm_ref, cw_ref,
                  ya_ref, qt_ref, k_ref, vt_ref, *, tiles_per_seq):
    t = pl.program_id(0) % tiles_per_seq
    g = g_ref[...]
    h = _rms(x_ref[...], g).astype(_BF16)

    wc = wc_ref[...]
    pc = _dot(h, wc)
    cu = pc[:, 2 * CONV_CH:] * pc[:, :CONV_CH]
    bg = pc[:, CONV_CH:2 * CONV_CH]

    def halo_row(ref, r, edge):
        ph = _dot(_rms(ref[...], g).astype(_BF16), wc)
        cu_h = ph[:, 2 * CONV_CH:] * ph[:, :CONV_CH]
        return jnp.where(edge, 0.0, cu_h[r:r + 1, :])

    prev_row = halo_row(xp_ref, xp_ref.shape[0] - 1, t == 0)
    next_row = halo_row(xn_ref, 0, t == tiles_per_seq - 1)
    row = lax.broadcasted_iota(jnp.int32, cu.shape, 0)
    cu_m1 = jnp.where(row == 0, prev_row, pltpu.roll(cu, 1, 0))
    cu_p1 = jnp.where(row == TILE - 1, next_row, pltpu.roll(cu, TILE - 1, 0))
    cw = cw_ref[...]
    ya = bg * (cw[0:1, :] * cu_m1 + cw[1:2, :] * cu + cw[2:3, :] * cu_p1)
    ya_ref[...] = ya.astype(_BF16)

    def head_norm(blk, hd, gain_ref):
        gm = gm_ref[0] if hd < DIFF_HEADS else gm_ref[1]
        ms = _dot((blk * blk).astype(_BF16), gm)
        return blk * lax.rsqrt(ms + EPS) * gain_ref[:, hd * LANES:(hd + 1) * LANES]

    qf = _dot(h, wq_ref[...])
    for hd in range(N_HEADS):
        qn = head_norm(qf[:, hd * LANES:(hd + 1) * LANES], hd, gq_ref)
        qt_ref[0, hd, 0] = qn.T.astype(_BF16)
    kf = _dot(h, wk_ref[...])
    for hd in range(N_HEADS):
        k_ref[0, hd] = head_norm(kf[:, hd * LANES:(hd + 1) * LANES], hd, gk_ref).astype(_BF16)
    vf = _dot(h, wv_ref[...])
    lane = lax.broadcasted_iota(jnp.int32, (TILE, LANES), 1)
    for hd in range(N_HEADS):
        vh = jnp.where(lane == HEAD_DIM, 1.0, vf[:, hd * LANES:(hd + 1) * LANES])
        vt_ref[0, hd, 0] = vh.T.astype(_BF16)


def _mixin(x2d, g, wc, wq, wk, wv, gq, gk, gm, cw, *, batch, seq):
    tps = seq // TILE
    halo = 8
    hpt = TILE // halo
    n_halo = x2d.shape[0] // halo
    pw = N_HEADS * LANES
    row = pl.BlockSpec((TILE, D_MODEL), lambda i: (i, 0))
    prev = pl.BlockSpec((halo, D_MODEL), lambda i: (jnp.maximum(i * hpt - 1, 0), 0))
    nxt = pl.BlockSpec((halo, D_MODEL), lambda i: (jnp.minimum((i + 1) * hpt, n_halo - 1), 0))
    head_rows = pl.BlockSpec((1, N_HEADS, TILE, LANES), lambda i: (i // tps, 0, i % tps, 0))
    head_cols = pl.BlockSpec((1, N_HEADS, 1, LANES, TILE), lambda i: (i // tps, 0, i % tps, 0, 0))
    return pl.pallas_call(
        functools.partial(_mixin_kernel, tiles_per_seq=tps),
        grid=(batch * tps,),
        in_specs=[row, prev, nxt, _resident((1, D_MODEL)), _resident((D_MODEL, 3 * CONV_CH)),
                  _resident((D_MODEL, pw)), _resident((D_MODEL, pw)), _resident((D_MODEL, pw)),
                  _resident((1, pw)), _resident((1, pw)), _resident((2, LANES, LANES)),
                  _resident((3, CONV_CH))],
        out_specs=[pl.BlockSpec((TILE, CONV_CH), lambda i: (i, 0)), head_cols, head_rows, head_cols],
        out_shape=[jax.ShapeDtypeStruct((batch * seq, CONV_CH), _BF16),
                   jax.ShapeDtypeStruct((batch, N_HEADS, tps, LANES, TILE), _BF16),
                   jax.ShapeDtypeStruct((batch, N_HEADS, seq, LANES), _BF16),
                   jax.ShapeDtypeStruct((batch, N_HEADS, tps, LANES, TILE), _BF16)],
        compiler_params=_params("parallel"),
        name="mixer_in",
    )(x2d, x2d, x2d, g, wc, wq, wk, wv, gq, gk, gm, cw)


def _strip_kernel(tab_ref, ref_ref, o_ref, *, band, head_off, dilated):
    hd = pl.program_id(0) + head_off
    off = pl.program_id(1) - band
    key = lax.broadcasted_iota(jnp.int32, (TILE, TILE), 0)
    qry = lax.broadcasted_iota(jnp.int32, (TILE, TILE), 1)
    rel = off * TILE + key - qry
    n = jnp.abs(rel)
    large = jnp.full(rel.shape, N_BUCKETS // 4, jnp.int32)
    for step in BUCKET_STEPS:
        large = large + jnp.where(n >= step, 1, 0)
    bucket = jnp.where(rel > 0, N_BUCKETS // 2, 0) + jnp.where(n < N_BUCKETS // 4, n, large)
    val = jnp.zeros(rel.shape, _F32)
    for b in range(N_BUCKETS):
        val = jnp.where(bucket == b, tab_ref[b, hd], val)
    if dilated:
        cnt = jnp.zeros(rel.shape, _F32)
        for window, dil in DIL_PAIRS:
            on_grid = jnp.where((n & (dil - 1)) == 0, 1.0, 0.0)
            cnt = cnt + jnp.where(n <= window // 2, on_grid, 0.0)
        val = val + jnp.log(jnp.maximum(cnt, 1.0))
    val = (val - ref_ref[0, hd]) * ref_ref[1, hd]
    if dilated:
        masked = jnp.logical_or(cnt == 0.0, pl.program_id(1) == 2 * band + 1)
        val = jnp.where(masked, NEG, val)
    o_ref[0, 0] = val


def _bias_strips(rel_bias, score_ref, *, n_heads, band, head_off, dilated):
    n_off = 2 * band + (2 if dilated else 1)
    return pl.pallas_call(
        functools.partial(_strip_kernel, band=band, head_off=head_off, dilated=dilated),
        grid=(n_heads, n_off),
        in_specs=[pl.BlockSpec(memory_space=pltpu.SMEM), pl.BlockSpec(memory_space=pltpu.SMEM)],
        out_specs=pl.BlockSpec((1, 1, TILE, TILE), lambda h, o: (h, o, 0, 0)),
        out_shape=jax.ShapeDtypeStruct((n_heads, n_off, TILE, TILE), _F32),
        compiler_params=_params("parallel", "parallel"),
        name="bias_strips_dil" if dilated else "bias_strips_diff",
    )(rel_bias, score_ref)


def _attn_kernel(qt_ref, k_ref, vt_ref, strip_ref, gsub_ref, lamv_ref, o_ref, m_ref, acc_ref, s_ref, *,
                 diff, band, n_tiles, lam_init, static_ref):
    i = pl.program_id(2)
    qt = qt_ref[0, 0, 0]
    feat = lax.broadcasted_iota(jnp.int32, (LANES, TILE), 0)
    if diff:
        zero = jnp.zeros_like(qt)
        qs = [jnp.where(feat < DIFF_QK_DIM, qt, zero), jnp.where(feat >= DIFF_QK_DIM, qt, zero)]
    else:
        qs = [qt]
    n_sub = len(qs)
    m_ref[...] = jnp.full(m_ref.shape, NEG, _F32)
    acc_ref[...] = jnp.zeros(acc_ref.shape, _F32)

    def body(j, carry):
        k = k_ref[0, 0, pl.ds(pl.multiple_of(j * TILE, TILE), TILE), :]
        vt = vt_ref[0, 0, j, :V_ROWS, :]
        bias = strip_ref[0, jnp.clip(j - i, -band, band) + band]
        for c in range(n_sub):
            s = _dot(k, qs[c]) + bias
            if static_ref:
                acc_ref[c, :V_ROWS, :] += _dot(vt, jnp.exp2(s).astype(_BF16))
                continue
            m_old = m_ref[c]
            m_new = jnp.maximum(m_old, jnp.max(s, axis=0, keepdims=True))
            alpha = jnp.exp(m_old - m_new)
            p = jnp.exp(s - m_new)
            acc_ref[c, :V_ROWS, :] = alpha * acc_ref[c, :V_ROWS, :] + _dot(vt, p.astype(_BF16))
            m_ref[c] = m_new
        return carry

    def scores(j, slot):
        k = k_ref[0, 0, pl.ds(pl.multiple_of(j * TILE, TILE), TILE), :]
        for c in range(n_sub):
            s_ref[slot, c] = _dot(k, qs[c])

    def consume(j, slot, strip_index):
        vt = vt_ref[0, 0, j, :V_ROWS, :]
        bias = strip_ref[0, strip_index]
        for c in range(n_sub):
            acc_ref[c, :V_ROWS, :] += _dot(vt, jnp.exp2(s_ref[slot, c] + bias).astype(_BF16))

    def group(jj, carry):
        j0 = KEY_TILES_PER_STEP * jj
        for u in range(KEY_TILES_PER_STEP):
            j = j0 + u
            scores(jnp.minimum(j + 1, n_tiles - 1), (u + 1) % 2)
            consume(j, u % 2, jnp.clip(j - i, -band, band) + band)
        return carry

    if static_ref and diff and n_tiles % KEY_TILES_PER_STEP == 0:
        scores(0, 0)
        lax.fori_loop(0, n_tiles // KEY_TILES_PER_STEP, group, 0)
    elif static_ref and not diff:
        offsets = list(range(-band, band + 1))
        tiles = [jnp.clip(i + o, 0, n_tiles - 1) for o in offsets]
        scores(tiles[0], 0)
        for u, o in enumerate(offsets):
            if u + 1 < len(offsets):
                scores(tiles[u + 1], (u + 1) % 2)
            in_range = jnp.logical_and(i + o >= 0, i + o < n_tiles)
            consume(tiles[u], u % 2, jnp.where(in_range, o + band, 2 * band + 1))
    elif diff:
        lax.fori_loop(0, n_tiles, body, 0)
    else:
        lax.fori_loop(jnp.maximum(i - band, 0), jnp.minimum(i + band + 1, n_tiles), body, 0)

    outs = []
    for c in range(n_sub):
        a = acc_ref[c]
        outs.append(a / a[HEAD_DIM:HEAD_DIM + 1, :])
    if diff:
        lamv = lamv_ref[...]
        lam = (jnp.exp(jnp.sum(lamv[0:1] * lamv[1:2], axis=-1, keepdims=True))
               - jnp.exp(jnp.sum(lamv[2:3] * lamv[3:4], axis=-1, keepdims=True)) + lam_init)
        o = outs[0] - lam * outs[1]
        o = jnp.where(feat < HEAD_DIM, o, 0.0)
        ms = jnp.sum(o * o, axis=0, keepdims=True) * (1.0 / HEAD_DIM)
        y = o * lax.rsqrt(ms + EPS) * gsub_ref[...] * (1.0 - lam_init)
    else:
        y = jnp.where(feat < HEAD_DIM, outs[0], 0.0)
    o_ref[0, 0] = y.T.astype(_BF16)


def _attention(qt, k, vt, strips, gsub, lamv, *, diff, lam_init, static_ref):
    batch, _, seq, _ = k.shape
    n_tiles = seq // TILE
    n_heads = DIFF_HEADS if diff else DIL_HEADS
    head_off = 0 if diff else DIFF_HEADS
    band = DIFF_BAND if diff else DIL_BAND
    n_off = strips.shape[1]
    n_sub = 2 if diff else 1
    return pl.pallas_call(
        functools.partial(_attn_kernel, diff=diff, band=band, n_tiles=n_tiles, lam_init=lam_init,
                          static_ref=static_ref),
        grid=(batch, n_heads, n_tiles),
        in_specs=[pl.BlockSpec((1, 1, 1, LANES, TILE), lambda b, h, i: (b, h + head_off, i, 0, 0)),
                  pl.BlockSpec((1, 1, seq, LANES), lambda b, h, i: (b, h + head_off, 0, 0)),
                  pl.BlockSpec((1, 1, n_tiles, LANES, TILE), lambda b, h, i: (b, h + head_off, 0, 0, 0)),
                  pl.BlockSpec((1, n_off, TILE, TILE), lambda b, h, i: (h, 0, 0, 0)),
                  pl.BlockSpec((LANES, 1), lambda b, h, i: (0, 0)),
                  pl.BlockSpec((4, DIFF_QK_DIM), lambda b, h, i: (0, 0))],
        out_specs=pl.BlockSpec((1, 1, TILE, LANES), lambda b, h, i: (b, h, i, 0)),
        out_shape=jax.ShapeDtypeStruct((batch, n_heads, seq, LANES), _BF16),
        scratch_shapes=[pltpu.VMEM((n_sub, 1, TILE), _F32), pltpu.VMEM((n_sub, LANES, TILE), _F32),
                        pltpu.VMEM((2, n_sub, TILE, TILE), _F32)],
        compiler_params=_params("parallel", "parallel", "arbitrary"),
        name=("diff_attention" if diff else "dilated_attention") + ("" if static_ref else "_online"),
    )(qt, k, vt, strips, gsub, lamv)


def _mixout_kernel(x_ref, ya_ref, yb_ref, yc_ref, wa_ref, wh_ref, o_ref):
    def head_pair(ref, g):
        lower = ref[0, 2 * g].astype(_F32)
        upper = pltpu.roll(ref[0, 2 * g + 1].astype(_F32), HEAD_DIM, 1)
        return (lower + upper).astype(_BF16)

    acc = _dot(ya_ref[...], wa_ref[...])
    for g in range(DIFF_HEADS // 2):
        acc = acc + _dot(head_pair(yb_ref, g), wh_ref[g])
    for g in range(DIL_HEADS // 2):
        acc = acc + _dot(head_pair(yc_ref, g), wh_ref[DIFF_HEADS // 2 + g])
    o_ref[...] = x_ref[...] + acc


def _mixout(x2d, ya, yb, yc, wa, wh, *, batch, seq):
    tps = seq // TILE
    row = pl.BlockSpec((TILE, D_MODEL), lambda i: (i, 0))
    return pl.pallas_call(
        _mixout_kernel,
        grid=(batch * tps,),
        in_specs=[row, pl.BlockSpec((TILE, CONV_CH), lambda i: (i, 0)),
                  pl.BlockSpec((1, DIFF_HEADS, TILE, LANES), lambda i: (i // tps, 0, i % tps, 0)),
                  pl.BlockSpec((1, DIL_HEADS, TILE, LANES), lambda i: (i // tps, 0, i % tps, 0)),
                  _resident((CONV_CH, D_MODEL)), _resident((N_HEADS // 2, LANES, D_MODEL))],
        out_specs=row,
        out_shape=jax.ShapeDtypeStruct(x2d.shape, _F32),
        compiler_params=_params("parallel"),
        name="mixer_out",
    )(x2d, ya, yb, yc, wa, wh)


def _pad_heads(w, n_heads):
    w = w.reshape(w.shape[0], n_heads, HEAD_DIM)
    w = jnp.pad(w, ((0, 0), (0, 0), (0, LANES - HEAD_DIM)))
    return w.reshape(w.shape[0], n_heads * LANES)


def _head_gain(g_diff, g_dil, diff_scale, dil_scale):
    zeros = jnp.zeros((LANES - HEAD_DIM,), _F32)
    diff = jnp.concatenate([g_diff, g_diff, zeros]) * diff_scale
    dil = jnp.concatenate([g_dil, zeros]) * dil_scale
    return jnp.concatenate([jnp.tile(diff, DIFF_HEADS), jnp.tile(dil, DIL_HEADS)])[None, :]


def _group_mean_mats():
    r = jnp.arange(LANES)
    live = (r[:, None] < HEAD_DIM) & (r[None, :] < HEAD_DIM)
    same32 = (r[:, None] // DIFF_QK_DIM) == (r[None, :] // DIFF_QK_DIM)
    g_diff = jnp.where(live & same32, 1.0 / DIFF_QK_DIM, 0.0)
    g_dil = jnp.where(live, 1.0 / HEAD_DIM, 0.0)
    return jnp.stack([g_diff, g_dil]).astype(_BF16)


def _softmax_reference(diff_q_norm, diff_k_norm, dil_q_norm, dil_k_norm, rel_bias):
    def qk_bound(gq, gk, dim):
        return math.sqrt(dim) * jnp.max(jnp.abs(gq)) * jnp.max(jnp.abs(gk)) * NORM_SLACK

    qk = jnp.concatenate([jnp.full((DIFF_HEADS,), qk_bound(diff_q_norm, diff_k_norm, DIFF_QK_DIM)),
                          jnp.full((DIL_HEADS,), qk_bound(dil_q_norm, dil_k_norm, HEAD_DIM))])
    merged = jnp.concatenate([jnp.zeros((DIFF_HEADS,), _F32),
                              jnp.full((DIL_HEADS,), math.log(len(DIL_PAIRS)), _F32)])
    hi = jnp.max(rel_bias, axis=0) + merged
    lo = jnp.min(rel_bias, axis=0)
    upper = qk + hi
    ok = jnp.max(2.0 * qk + hi - lo) <= SAFE_LOGIT_SPAN
    return upper, ok


def kernel(x_prompt, x_sample, ffn1_norm, ffn1_w_gu, ffn1_w_down, mix_norm, w_in, conv_w, diff_q_norm, diff_k_norm,
           lambda_q1, lambda_k1, lambda_q2, lambda_k2, diff_sub_norm, dil_q_norm, dil_k_norm, w_out, ffn2_norm,
           ffn2_w_gu, ffn2_w_down, final_norm, rel_bias):
    n_prompt = x_prompt.shape[0]
    x = jnp.concatenate([x_prompt, x_sample], axis=0)
    batch, seq, _ = x.shape
    x = x.reshape(batch * seq, D_MODEL)

    upper, static_ok = _softmax_reference(diff_q_norm, diff_k_norm, dil_q_norm, dil_k_norm, rel_bias)
    logit_scale = jnp.where(static_ok, LOG2E, 1.0).astype(_F32)
    logit_shift = jnp.stack([jnp.where(static_ok, upper, 0.0), jnp.full((N_HEADS,), logit_scale)]).astype(_F32)
    strips_diff = _bias_strips(rel_bias, logit_shift, n_heads=DIFF_HEADS, band=DIFF_BAND, head_off=0,
                               dilated=False)
    strips_dil = _bias_strips(rel_bias, logit_shift, n_heads=DIL_HEADS, band=DIL_BAND, head_off=DIFF_HEADS,
                              dilated=True)
    gm = _group_mean_mats()
    zeros_gain = jnp.zeros((LANES, 1), _F32)
    c0 = 3 * CONV_CH
    c1 = c0 + 3 * DIFF_HEADS * HEAD_DIM

    for l in range(DEPTH):
        lam_init = 0.8 - 0.6 * math.exp(-0.3 * l)

        def ffn(xin, norm, w_gu, w_down, final):
            return _ffn(xin, norm[l][None, :], w_gu[l][:, :D_FF].astype(_BF16), w_gu[l][:, D_FF:].astype(_BF16),
                        w_down[l].astype(_BF16), final_norm[l][None, :], final_norm=final)

        x = ffn(x, ffn1_norm, ffn1_w_gu, ffn1_w_down, False)

        wl = w_in[l]
        dw = DIFF_HEADS * HEAD_DIM
        lw = DIL_HEADS * HEAD_DIM
        wq = jnp.concatenate([_pad_heads(wl[:, c0:c0 + dw], DIFF_HEADS),
                              _pad_heads(wl[:, c1:c1 + lw], DIL_HEADS)], axis=1).astype(_BF16)
        wk = jnp.concatenate([_pad_heads(wl[:, c0 + dw:c0 + 2 * dw], DIFF_HEADS),
                              _pad_heads(wl[:, c1 + lw:c1 + 2 * lw], DIL_HEADS)], axis=1).astype(_BF16)
        wv = jnp.concatenate([_pad_heads(wl[:, c0 + 2 * dw:c0 + 3 * dw], DIFF_HEADS),
                              _pad_heads(wl[:, c1 + 2 * lw:c1 + 3 * lw], DIL_HEADS)], axis=1).astype(_BF16)
        gq = _head_gain(diff_q_norm[l], dil_q_norm[l], DIFF_QK_DIM ** -0.5, HEAD_DIM ** -0.5) * logit_scale
        gk = _head_gain(diff_k_norm[l], dil_k_norm[l], 1.0, 1.0)
        ya, qt, k, vt = _mixin(x, mix_norm[l][None, :], wl[:, :c0].astype(_BF16), wq, wk, wv, gq, gk, gm, conv_w[l],
                              batch=batch, seq=seq)

        gsub = jnp.concatenate([diff_sub_norm[l], jnp.zeros((LANES - HEAD_DIM,), _F32)])[:, None]
        lamv = jnp.stack([lambda_q1[l], lambda_k1[l], lambda_q2[l], lambda_k2[l]]).astype(_F32)
        def attend(strips, gain, diff):
            branches = [functools.partial(_attention, diff=diff, lam_init=lam_init, static_ref=flag)
                        for flag in (True, False)]
            return lax.cond(static_ok, *branches, qt, k, vt, strips, gain, lamv)

        yb = attend(strips_diff, gsub, True)
        yc = attend(strips_dil, zeros_gain, False)

        wo = w_out[l]
        wh = wo[CONV_CH:].reshape(N_HEADS // 2, LANES, D_MODEL).astype(_BF16)
        x = _mixout(x, ya, yb, yc, wo[:CONV_CH].astype(_BF16), wh, batch=batch, seq=seq)

        x = ffn(x, ffn2_norm, ffn2_w_gu, ffn2_w_down, True)

    x = x.reshape(batch, seq, D_MODEL)
    return (x[:n_prompt], x[n_prompt:])
```

```python
import functools
import math

import jax
import jax.numpy as jnp
from jax import lax
from jax.experimental import pallas as pl
from jax.experimental.pallas import tpu as pltpu

D_MODEL = 1024
D_FF = 2816
DEPTH = 2
EPS = 1e-6
CONV_CH = 384
DIFF_HEADS = 4
DIFF_QK_DIM = 32
DIL_HEADS = 6
HEAD_DIM = 64
N_HEADS = DIFF_HEADS + DIL_HEADS
N_BUCKETS = 32
BUCKET_STEPS = (15, 27, 50, 91, 166, 305, 559)
DIL_PAIRS = ((128, 1), (512, 4), (2048, 16))

LANES = 128
TILE = 512
FFN_CHUNKS = 2
V_ROWS = 80
KEY_TILES_PER_STEP = 16
DIFF_BAND = 3
DIL_BAND = 2
NEG = -1e30
LOG2E = math.log2(math.e)
SAFE_LOGIT_SPAN = 60.0
NORM_SLACK = 1.01
VMEM_LIMIT_BYTES = 56 * 1024 * 1024

_BF16 = jnp.bfloat16
_F32 = jnp.float32


def _params(*semantics):
    return pltpu.CompilerParams(dimension_semantics=semantics, vmem_limit_bytes=VMEM_LIMIT_BYTES)


def _dot(a, b):
    return jnp.dot(a, b, preferred_element_type=_F32)


def _rms(x, g):
    ms = jnp.mean(x * x, axis=-1, keepdims=True)
    return x * lax.rsqrt(ms + EPS) * g


def _resident(shape):
    zeros = (0,) * len(shape)
    return pl.BlockSpec(shape, lambda *_: zeros, pipeline_mode=pl.Buffered(1))


def _ffn_kernel(x_ref, g_ref, wg_ref, wu_ref, wd_ref, fg_ref, o_ref, *, final_norm):
    x = x_ref[...]
    h = _rms(x, g_ref[...]).astype(_BF16)
    fc = D_FF // FFN_CHUNKS
    acc = jnp.zeros(x.shape, _F32)
    for c in range(FFN_CHUNKS):
        g = _dot(h, wg_ref[:, c * fc:(c + 1) * fc])
        u = _dot(h, wu_ref[:, c * fc:(c + 1) * fc])
        a = (g * jax.nn.sigmoid(g) * u).astype(_BF16)
        acc = acc + _dot(a, wd_ref[c * fc:(c + 1) * fc, :])
    y = x + 0.5 * acc
    if final_norm:
        y = _rms(y, fg_ref[...])
    o_ref[...] = y


def _ffn(x2d, g, wg, wu, wd, fg, *, final_norm):
    t = x2d.shape[0]
    row = pl.BlockSpec((TILE, D_MODEL), lambda i: (i, 0))
    return pl.pallas_call(
        functools.partial(_ffn_kernel, final_norm=final_norm),
        grid=(t // TILE,),
        in_specs=[row, _resident((1, D_MODEL)), _resident((D_MODEL, D_FF)), _resident((D_MODEL, D_FF)),
                  _resident((D_FF, D_MODEL)), _resident((1, D_MODEL))],
        out_specs=row,
        out_shape=jax.ShapeDtypeStruct((t, D_MODEL), _F32),
        compiler_params=_params("parallel"),
        name="ffn_final" if final_norm else "ffn",
    )(x2d, g, wg, wu, wd, fg)


def _mixin_kernel(x_ref, xp_ref, xn_ref, g_ref, wc_ref, wq_ref, wk_ref, wv_ref, gq_ref, gk_ref, gm_ref, cw_ref,
                  ya_ref, qt_ref, k_ref, vt_ref, *, tiles_per_seq):
    t = pl.program_id(0) % tiles_per_seq
    g = g_ref[...]
    h = _rms(x_ref[...], g).astype(_BF16)

    wc = wc_ref[...]
    pc = _dot(h, wc)
    cu = pc[:, 2 * CONV_CH:] * pc[:, :CONV_CH]
    bg = pc[:, CONV_CH:2 * CONV_CH]

    def halo_row(ref, r, edge):
        ph = _dot(_rms(ref[...], g).astype(_BF16), wc)
        cu_h = ph[:, 2 * CONV_CH:] * ph[:, :CONV_CH]
        return jnp.where(edge, 0.0, cu_h[r:r + 1, :])

    prev_row = halo_row(xp_ref, xp_ref.shape[0] - 1, t == 0)
    next_row = halo_row(xn_ref, 0, t == tiles_per_seq - 1)
    row = lax.broadcasted_iota(jnp.int32, cu.shape, 0)
    cu_m1 = jnp.where(row == 0, prev_row, pltpu.roll(cu, 1, 0))
    cu_p1 = jnp.where(row == TILE - 1, next_row, pltpu.roll(cu, TILE - 1, 0))
    cw = cw_ref[...]
    ya = bg * (cw[0:1, :] * cu_m1 + cw[1:2, :] * cu + cw[2:3, :] * cu_p1)
    ya_ref[...] = ya.astype(_BF16)

    def head_norm(blk, hd, gain_ref):
        gm = gm_ref[0] if hd < DIFF_HEADS else gm_ref[1]
        ms = _dot((blk * blk).astype(_BF16), gm)
        return blk * lax.rsqrt(ms + EPS) * gain_ref[:, hd * LANES:(hd + 1) * LANES]

    qf = _dot(h, wq_ref[...])
    for hd in range(N_HEADS):
        qn = head_norm(qf[:, hd * LANES:(hd + 1) * LANES], hd, gq_ref)
        qt_ref[0, hd, 0] = qn.T.astype(_BF16)
    kf = _dot(h, wk_ref[...])
    for hd in range(N_HEADS):
        k_ref[0, hd] = head_norm(kf[:, hd * LANES:(hd + 1) * LANES], hd, gk_ref).astype(_BF16)
    vf = _dot(h, wv_ref[...])
    lane = lax.broadcasted_iota(jnp.int32, (TILE, LANES), 1)
    for hd in range(N_HEADS):
        vh = jnp.where(lane == HEAD_DIM, 1.0, vf[:, hd * LANES:(hd + 1) * LANES])
        vt_ref[0, hd, 0] = vh.T.astype(_BF16)


def _mixin(x2d, g, wc, wq, wk, wv, gq, gk, gm, cw, *, batch, seq):
    tps = seq // TILE
    halo = 8
    hpt = TILE // halo
    n_halo = x2d.shape[0] // halo
    pw = N_HEADS * LANES
    row = pl.BlockSpec((TILE, D_MODEL), lambda i: (i, 0))
    prev = pl.BlockSpec((halo, D_MODEL), lambda i: (jnp.maximum(i * hpt - 1, 0), 0))
    nxt = pl.BlockSpec((halo, D_MODEL), lambda i: (jnp.minimum((i + 1) * hpt, n_halo - 1), 0))
    head_rows = pl.BlockSpec((1, N_HEADS, TILE, LANES), lambda i: (i // tps, 0, i % tps, 0))
    head_cols = pl.BlockSpec((1, N_HEADS, 1, LANES, TILE), lambda i: (i // tps, 0, i % tps, 0, 0))
    return pl.pallas_call(
        functools.partial(_mixin_kernel, tiles_per_seq=tps),
        grid=(batch * tps,),
        in_specs=[row, prev, nxt, _resident((1, D_MODEL)), _resident((D_MODEL, 3 * CONV_CH)),
                  _resident((D_MODEL, pw)), _resident((D_MODEL, pw)), _resident((D_MODEL, pw)),
                  _resident((1, pw)), _resident((1, pw)), _resident((2, LANES, LANES)),
                  _resident((3, CONV_CH))],
        out_specs=[pl.BlockSpec((TILE, CONV_CH), lambda i: (i, 0)), head_cols, head_rows, head_cols],
        out_shape=[jax.ShapeDtypeStruct((batch * seq, CONV_CH), _BF16),
                   jax.ShapeDtypeStruct((batch, N_HEADS, tps, LANES, TILE), _BF16),
                   jax.ShapeDtypeStruct((batch, N_HEADS, seq, LANES), _BF16),
                   jax.ShapeDtypeStruct((batch, N_HEADS, tps, LANES, TILE), _BF16)],
        compiler_params=_params("parallel"),
        name="mixer_in",
    )(x2d, x2d, x2d, g, wc, wq, wk, wv, gq, gk, gm, cw)


def _strip_kernel(tab_ref, ref_ref, o_ref, *, band, head_off, dilated):
    hd = pl.program_id(0) + head_off
    off = pl.program_id(1) - band
    key = lax.broadcasted_iota(jnp.int32, (TILE, TILE), 0)
    qry = lax.broadcasted_iota(jnp.int32, (TILE, TILE), 1)
    rel = off * TILE + key - qry
    n = jnp.abs(rel)
    large = jnp.full(rel.shape, N_BUCKETS // 4, jnp.int32)
    for step in BUCKET_STEPS:
        large = large + jnp.where(n >= step, 1, 0)
    bucket = jnp.where(rel > 0, N_BUCKETS // 2, 0) + jnp.where(n < N_BUCKETS // 4, n, large)
    val = jnp.zeros(rel.shape, _F32)
    for b in range(N_BUCKETS):
        val = jnp.where(bucket == b, tab_ref[b, hd], val)
    if dilated:
        cnt = jnp.zeros(rel.shape, _F32)
        for window, dil in DIL_PAIRS:
            on_grid = jnp.where((n & (dil - 1)) == 0, 1.0, 0.0)
            cnt = cnt + jnp.where(n <= window // 2, on_grid, 0.0)
        val = val + jnp.log(jnp.maximum(cnt, 1.0))
    val = (val - ref_ref[0, hd]) * ref_ref[1, hd]
    if dilated:
        masked = jnp.logical_or(cnt == 0.0, pl.program_id(1) == 2 * band + 1)
        val = jnp.where(masked, NEG, val)
    o_ref[0, 0] = val


def _bias_strips(rel_bias, score_ref, *, n_heads, band, head_off, dilated):
    n_off = 2 * band + (2 if dilated else 1)
    return pl.pallas_call(
        functools.partial(_strip_kernel, band=band, head_off=head_off, dilated=dilated),
        grid=(n_heads, n_off),
        in_specs=[pl.BlockSpec(memory_space=pltpu.SMEM), pl.BlockSpec(memory_space=pltpu.SMEM)],
        out_specs=pl.BlockSpec((1, 1, TILE, TILE), lambda h, o: (h, o, 0, 0)),
        out_shape=jax.ShapeDtypeStruct((n_heads, n_off, TILE, TILE), _F32),
        compiler_params=_params("parallel", "parallel"),
        name="bias_strips_dil" if dilated else "bias_strips_diff",
    )(rel_bias, score_ref)


def _attn_kernel(qt_ref, k_ref, vt_ref, strip_ref, gsub_ref, lamv_ref, o_ref, m_ref, acc_ref, s_ref, *,
                 diff, band, n_tiles, lam_init, static_ref):
    i = pl.program_id(2)
    qt = qt_ref[0, 0, 0]
    feat = lax.broadcasted_iota(jnp.int32, (LANES, TILE), 0)
    if diff:
        zero = jnp.zeros_like(qt)
        qs = [jnp.where(feat < DIFF_QK_DIM, qt, zero), jnp.where(feat >= DIFF_QK_DIM, qt, zero)]
    else:
        qs = [qt]
    n_sub = len(qs)
    m_ref[...] = jnp.full(m_ref.shape, NEG, _F32)
    acc_ref[...] = jnp.zeros(acc_ref.shape, _F32)

    def body(j, carry):
        k = k_ref[0, 0, pl.ds(pl.multiple_of(j * TILE, TILE), TILE), :]
        vt = vt_ref[0, 0, j, :V_ROWS, :]
        bias = strip_ref[0, jnp.clip(j - i, -band, band) + band]
        for c in range(n_sub):
            s = _dot(k, qs[c]) + bias
            if static_ref:
                acc_ref[c, :V_ROWS, :] += _dot(vt, jnp.exp2(s).astype(_BF16))
                continue
            m_old = m_ref[c]
            m_new = jnp.maximum(m_old, jnp.max(s, axis=0, keepdims=True))
            alpha = jnp.exp(m_old - m_new)
            p = jnp.exp(s - m_new)
            acc_ref[c, :V_ROWS, :] = alpha * acc_ref[c, :V_ROWS, :] + _dot(vt, p.astype(_BF16))
            m_ref[c] = m_new
        return carry

    def scores(j, slot):
        k = k_ref[0, 0, pl.ds(pl.multiple_of(j * TILE, TILE), TILE), :]
        for c in range(n_sub):
            s_ref[slot, c] = _dot(k, qs[c])

    def consume(j, slot, strip_index):
        vt = vt_ref[0, 0, j, :V_ROWS, :]
        bias = strip_ref[0, strip_index]
        for c in range(n_sub):
            acc_ref[c, :V_ROWS, :] += _dot(vt, jnp.exp2(s_ref[slot, c] + bias).astype(_BF16))

    def group(j0, final):
        for u in range(KEY_TILES_PER_STEP):
            j = j0 + u
            if not (final and u == KEY_TILES_PER_STEP - 1):
                scores(j + 1, (u + 1) % 2)
            consume(j, u % 2, jnp.clip(j - i, -band, band) + band)

    def group_step(jj, carry):
        group(KEY_TILES_PER_STEP * jj, False)
        return carry

    if static_ref and diff and n_tiles % KEY_TILES_PER_STEP == 0:
        n_groups = n_tiles // KEY_TILES_PER_STEP
        scores(0, 0)
        lax.fori_loop(0, n_groups - 1, group_step, 0)
        group(KEY_TILES_PER_STEP * (n_groups - 1), True)
    elif static_ref and not diff:
        offsets = list(range(-band, band + 1))
        tiles = [jnp.clip(i + o, 0, n_tiles - 1) for o in offsets]
        scores(tiles[0], 0)
        for u, o in enumerate(offsets):
            if u + 1 < len(offsets):
                scores(tiles[u + 1], (u + 1) % 2)
            in_range = jnp.logical_and(i + o >= 0, i + o < n_tiles)
            consume(tiles[u], u % 2, jnp.where(in_range, o + band, 2 * band + 1))
    elif diff:
        lax.fori_loop(0, n_tiles, body, 0)
    else:
        lax.fori_loop(jnp.maximum(i - band, 0), jnp.minimum(i + band + 1, n_tiles), body, 0)

    outs = []
    for c in range(n_sub):
        a = acc_ref[c]
        outs.append(a / a[HEAD_DIM:HEAD_DIM + 1, :])
    if diff:
        lamv = lamv_ref[...]
        lam = (jnp.exp(jnp.sum(lamv[0:1] * lamv[1:2], axis=-1, keepdims=True))
               - jnp.exp(jnp.sum(lamv[2:3] * lamv[3:4], axis=-1, keepdims=True)) + lam_init)
        o = outs[0] - lam * outs[1]
        o = jnp.where(feat < HEAD_DIM, o, 0.0)
        ms = jnp.sum(o * o, axis=0, keepdims=True) * (1.0 / HEAD_DIM)
        y = o * lax.rsqrt(ms + EPS) * gsub_ref[...] * (1.0 - lam_init)
    else:
        y = jnp.where(feat < HEAD_DIM, outs[0], 0.0)
    o_ref[0, 0] = y.T.astype(_BF16)


def _attention(qt, k, vt, strips, gsub, lamv, *, diff, lam_init, static_ref):
    batch, _, seq, _ = k.shape
    n_tiles = seq // TILE
    n_heads = DIFF_HEADS if diff else DIL_HEADS
    head_off = 0 if diff else DIFF_HEADS
    band = DIFF_BAND if diff else DIL_BAND
    n_off = strips.shape[1]
    n_sub = 2 if diff else 1
    return pl.pallas_call(
        functools.partial(_attn_kernel, diff=diff, band=band, n_tiles=n_tiles, lam_init=lam_init,
                          static_ref=static_ref),
        grid=(batch, n_heads, n_tiles),
        in_specs=[pl.BlockSpec((1, 1, 1, LANES, TILE), lambda b, h, i: (b, h + head_off, i, 0, 0)),
                  pl.BlockSpec((1, 1, seq, LANES), lambda b, h, i: (b, h + head_off, 0, 0)),
                  pl.BlockSpec((1, 1, n_tiles, LANES, TILE), lambda b, h, i: (b, h + head_off, 0, 0, 0)),
                  pl.BlockSpec((1, n_off, TILE, TILE), lambda b, h, i: (h, 0, 0, 0)),
                  pl.BlockSpec((LANES, 1), lambda b, h, i: (0, 0)),
                  pl.BlockSpec((4, DIFF_QK_DIM), lambda b, h, i: (0, 0))],
        out_specs=pl.BlockSpec((1, 1, TILE, LANES), lambda b, h, i: (b, h, i, 0)),
        out_shape=jax.ShapeDtypeStruct((batch, n_heads, seq, LANES), _BF16),
        scratch_shapes=[pltpu.VMEM((n_sub, 1, TILE), _F32), pltpu.VMEM((n_sub, LANES, TILE), _F32),
                        pltpu.VMEM((2, n_sub, TILE, TILE), _F32)],
        compiler_params=_params("parallel", "parallel", "arbitrary"),
        name=("diff_attention" if diff else "dilated_attention") + ("" if static_ref else "_online"),
    )(qt, k, vt, strips, gsub, lamv)


def _mixout_kernel(x_ref, ya_ref, yb_ref, yc_ref, wa_ref, wh_ref, o_ref):
    def head_pair(ref, g):
        lower = ref[0, 2 * g].astype(_F32)
        upper = pltpu.roll(ref[0, 2 * g + 1].astype(_F32), HEAD_DIM, 1)
        return (lower + upper).astype(_BF16)

    acc = _dot(ya_ref[...], wa_ref[...])
    for g in range(DIFF_HEADS // 2):
        acc = acc + _dot(head_pair(yb_ref, g), wh_ref[g])
    for g in range(DIL_HEADS // 2):
        acc = acc + _dot(head_pair(yc_ref, g), wh_ref[DIFF_HEADS // 2 + g])
    o_ref[...] = x_ref[...] + acc


def _mixout(x2d, ya, yb, yc, wa, wh, *, batch, seq):
    tps = seq // TILE
    row = pl.BlockSpec((TILE, D_MODEL), lambda i: (i, 0))
    return pl.pallas_call(
        _mixout_kernel,
        grid=(batch * tps,),
        in_specs=[row, pl.BlockSpec((TILE, CONV_CH), lambda i: (i, 0)),
                  pl.BlockSpec((1, DIFF_HEADS, TILE, LANES), lambda i: (i // tps, 0, i % tps, 0)),
                  pl.BlockSpec((1, DIL_HEADS, TILE, LANES), lambda i: (i // tps, 0, i % tps, 0)),
                  _resident((CONV_CH, D_MODEL)), _resident((N_HEADS // 2, LANES, D_MODEL))],
        out_specs=row,
        out_shape=jax.ShapeDtypeStruct(x2d.shape, _F32),
        compiler_params=_params("parallel"),
        name="mixer_out",
    )(x2d, ya, yb, yc, wa, wh)


def _pad_heads(w, n_heads):
    w = w.reshape(w.shape[0], n_heads, HEAD_DIM)
    w = jnp.pad(w, ((0, 0), (0, 0), (0, LANES - HEAD_DIM)))
    return w.reshape(w.shape[0], n_heads * LANES)


def _head_gain(g_diff, g_dil, diff_scale, dil_scale):
    zeros = jnp.zeros((LANES - HEAD_DIM,), _F32)
    diff = jnp.concatenate([g_diff, g_diff, zeros]) * diff_scale
    dil = jnp.concatenate([g_dil, zeros]) * dil_scale
    return jnp.concatenate([jnp.tile(diff, DIFF_HEADS), jnp.tile(dil, DIL_HEADS)])[None, :]


def _group_mean_mats():
    r = jnp.arange(LANES)
    live = (r[:, None] < HEAD_DIM) & (r[None, :] < HEAD_DIM)
    same32 = (r[:, None] // DIFF_QK_DIM) == (r[None, :] // DIFF_QK_DIM)
    g_diff = jnp.where(live & same32, 1.0 / DIFF_QK_DIM, 0.0)
    g_dil = jnp.where(live, 1.0 / HEAD_DIM, 0.0)
    return jnp.stack([g_diff, g_dil]).astype(_BF16)


def _softmax_reference(diff_q_norm, diff_k_norm, dil_q_norm, dil_k_norm, rel_bias):
    def qk_bound(gq, gk, dim):
        return math.sqrt(dim) * jnp.max(jnp.abs(gq)) * jnp.max(jnp.abs(gk)) * NORM_SLACK

    qk = jnp.concatenate([jnp.full((DIFF_HEADS,), qk_bound(diff_q_norm, diff_k_norm, DIFF_QK_DIM)),
                          jnp.full((DIL_HEADS,), qk_bound(dil_q_norm, dil_k_norm, HEAD_DIM))])
    merged = jnp.concatenate([jnp.zeros((DIFF_HEADS,), _F32),
                              jnp.full((DIL_HEADS,), math.log(len(DIL_PAIRS)), _F32)])
    hi = jnp.max(rel_bias, axis=0) + merged
    lo = jnp.min(rel_bias, axis=0)
    upper = qk + hi
    ok = jnp.max(2.0 * qk + hi - lo) <= SAFE_LOGIT_SPAN
    return upper, ok


def kernel(x_prompt, x_sample, ffn1_norm, ffn1_w_gu, ffn1_w_down, mix_norm, w_in, conv_w, diff_q_norm, diff_k_norm,
           lambda_q1, lambda_k1, lambda_q2, lambda_k2, diff_sub_norm, dil_q_norm, dil_k_norm, w_out, ffn2_norm,
           ffn2_w_gu, ffn2_w_down, final_norm, rel_bias):
    n_prompt = x_prompt.shape[0]
    x = jnp.concatenate([x_prompt, x_sample], axis=0)
    batch, seq, _ = x.shape
    x = x.reshape(batch * seq, D_MODEL)

    upper, static_ok = _softmax_reference(diff_q_norm, diff_k_norm, dil_q_norm, dil_k_norm, rel_bias)
    logit_scale = jnp.where(static_ok, LOG2E, 1.0).astype(_F32)
    logit_shift = jnp.stack([jnp.where(static_ok, upper, 0.0), jnp.full((N_HEADS,), logit_scale)]).astype(_F32)
    strips_diff = _bias_strips(rel_bias, logit_shift, n_heads=DIFF_HEADS, band=DIFF_BAND, head_off=0,
                               dilated=False)
    strips_dil = _bias_strips(rel_bias, logit_shift, n_heads=DIL_HEADS, band=DIL_BAND, head_off=DIFF_HEADS,
                              dilated=True)
    gm = _group_mean_mats()
    zeros_gain = jnp.zeros((LANES, 1), _F32)
    c0 = 3 * CONV_CH
    c1 = c0 + 3 * DIFF_HEADS * HEAD_DIM

    for l in range(DEPTH):
        lam_init = 0.8 - 0.6 * math.exp(-0.3 * l)

        def ffn(xin, norm, w_gu, w_down, final):
            return _ffn(xin, norm[l][None, :], w_gu[l][:, :D_FF].astype(_BF16), w_gu[l][:, D_FF:].astype(_BF16),
                        w_down[l].astype(_BF16), final_norm[l][None, :], final_norm=final)

        x = ffn(x, ffn1_norm, ffn1_w_gu, ffn1_w_down, False)

        wl = w_in[l]
        dw = DIFF_HEADS * HEAD_DIM
        lw = DIL_HEADS * HEAD_DIM
        wq = jnp.concatenate([_pad_heads(wl[:, c0:c0 + dw], DIFF_HEADS),
                              _pad_heads(wl[:, c1:c1 + lw], DIL_HEADS)], axis=1).astype(_BF16)
        wk = jnp.concatenate([_pad_heads(wl[:, c0 + dw:c0 + 2 * dw], DIFF_HEADS),
                              _pad_heads(wl[:, c1 + lw:c1 + 2 * lw], DIL_HEADS)], axis=1).astype(_BF16)
        wv = jnp.concatenate([_pad_heads(wl[:, c0 + 2 * dw:c0 + 3 * dw], DIFF_HEADS),
                              _pad_heads(wl[:, c1 + 2 * lw:c1 + 3 * lw], DIL_HEADS)], axis=1).astype(_BF16)
        gq = _head_gain(diff_q_norm[l], dil_q_norm[l], DIFF_QK_DIM ** -0.5, HEAD_DIM ** -0.5) * logit_scale
        gk = _head_gain(diff_k_norm[l], dil_k_norm[l], 1.0, 1.0)
        ya, qt, k, vt = _mixin(x, mix_norm[l][None, :], wl[:, :c0].astype(_BF16), wq, wk, wv, gq, gk, gm, conv_w[l],
                              batch=batch, seq=seq)

        gsub = jnp.concatenate([diff_sub_norm[l], jnp.zeros((LANES - HEAD_DIM,), _F32)])[:, None]
        lamv = jnp.stack([lambda_q1[l], lambda_k1[l], lambda_q2[l], lambda_k2[l]]).astype(_F32)
        def attend(strips, gain, diff):
            branches = [functools.partial(_attention, diff=diff, lam_init=lam_init, static_ref=flag)
                        for flag in (True, False)]
            return lax.cond(static_ok, *branches, qt, k, vt, strips, gain, lamv)

        yb = attend(strips_diff, gsub, True)
        yc = attend(strips_dil, zeros_gain, False)

        wo = w_out[l]
        wh = wo[CONV_CH:].reshape(N_HEADS // 2, LANES, D_MODEL).astype(_BF16)
        x = _mixout(x, ya, yb, yc, wo[:CONV_CH].astype(_BF16), wh, batch=batch, seq=seq)

        x = ffn(x, ffn2_norm, ffn2_w_gu, ffn2_w_down, True)

    x = x.reshape(batch, seq, D_MODEL)
    return (x[:n_prompt], x[n_prompt:])
```

```python
import functools
import math

import jax
import jax.numpy as jnp
from jax import lax
from jax.experimental import pallas as pl
from jax.experimental.pallas import tpu as pltpu

D_MODEL = 1024
D_FF = 2816
DEPTH = 2
EPS = 1e-6
CONV_CH = 384
DIFF_HEADS = 4
DIFF_QK_DIM = 32
DIL_HEADS = 6
HEAD_DIM = 64
N_HEADS = DIFF_HEADS + DIL_HEADS
N_BUCKETS = 32
BUCKET_STEPS = (15, 27, 50, 91, 166, 305, 559)
DIL_PAIRS = ((128, 1), (512, 4), (2048, 16))

LANES = 128
TILE = 512
FFN_CHUNKS = 1
V_ROWS = 80
KEY_TILES_PER_STEP = 16
DIFF_BAND = 3
DIL_BAND = 2
NEG = -1e30
LOG2E = math.log2(math.e)
SAFE_LOGIT_SPAN = 60.0
NORM_SLACK = 1.01
VMEM_LIMIT_BYTES = 56 * 1024 * 1024

_BF16 = jnp.bfloat16
_F32 = jnp.float32


def _params(*semantics):
    return pltpu.CompilerParams(dimension_semantics=semantics, vmem_limit_bytes=VMEM_LIMIT_BYTES)


def _dot(a, b):
    return jnp.dot(a, b, preferred_element_type=_F32)


def _rms(x, g):
    ms = jnp.mean(x * x, axis=-1, keepdims=True)
    return x * lax.rsqrt(ms + EPS) * g


def _resident(shape):
    zeros = (0,) * len(shape)
    return pl.BlockSpec(shape, lambda *_: zeros, pipeline_mode=pl.Buffered(1))


def _ffn_kernel(x_ref, g_ref, wg_ref, wu_ref, wd_ref, fg_ref, o_ref, *, final_norm):
    x = x_ref[...]
    h = _rms(x, g_ref[...]).astype(_BF16)
    fc = D_FF // FFN_CHUNKS
    acc = jnp.zeros(x.shape, _F32)
    for c in range(FFN_CHUNKS):
        g = _dot(h, wg_ref[:, c * fc:(c + 1) * fc])
        u = _dot(h, wu_ref[:, c * fc:(c + 1) * fc])
        a = (g * jax.nn.sigmoid(g) * u).astype(_BF16)
        acc = acc + _dot(a, wd_ref[c * fc:(c + 1) * fc, :])
    y = x + 0.5 * acc
    if final_norm:
        y = _rms(y, fg_ref[...])
    o_ref[...] = y


def _ffn(x2d, g, wg, wu, wd, fg, *, final_norm):
    t = x2d.shape[0]
    row = pl.BlockSpec((TILE, D_MODEL), lambda i: (i, 0))
    return pl.pallas_call(
        functools.partial(_ffn_kernel, final_norm=final_norm),
        grid=(t // TILE,),
        in_specs=[row, _resident((1, D_MODEL)), _resident((D_MODEL, D_FF)), _resident((D_MODEL, D_FF)),
                  _resident((D_FF, D_MODEL)), _resident((1, D_MODEL))],
        out_specs=row,
        out_shape=jax.ShapeDtypeStruct((t, D_MODEL), _F32),
        compiler_params=_params("parallel"),
        name="ffn_final" if final_norm else "ffn",
    )(x2d, g, wg, wu, wd, fg)


def _mixin_kernel(x_ref, xp_ref, xn_ref, g_ref, wc_ref, wq_ref, wk_ref, wv_ref, gq_ref, gk_ref, gm_ref, cw_ref,
                  ya_ref, qt_ref, k_ref, vt_ref, *, tiles_per_seq):
    t = pl.program_id(0) % tiles_per_seq
    g = g_ref[...]
    h = _rms(x_ref[...], g).astype(_BF16)

    wc = wc_ref[...]
    pc = _dot(h, wc)
    cu = pc[:, 2 * CONV_CH:] * pc[:, :CONV_CH]
    bg = pc[:, CONV_CH:2 * CONV_CH]

    def halo_row(ref, r, edge):
        ph = _dot(_rms(ref[...], g).astype(_BF16), wc)
        cu_h = ph[:, 2 * CONV_CH:] * ph[:, :CONV_CH]
        return jnp.where(edge, 0.0, cu_h[r:r + 1, :])

    prev_row = halo_row(xp_ref, xp_ref.shape[0] - 1, t == 0)
    next_row = halo_row(xn_ref, 0, t == tiles_per_seq - 1)
    row = lax.broadcasted_iota(jnp.int32, cu.shape, 0)
    cu_m1 = jnp.where(row == 0, prev_row, pltpu.roll(cu, 1, 0))
    cu_p1 = jnp.where(row == TILE - 1, next_row, pltpu.roll(cu, TILE - 1, 0))
    cw = cw_ref[...]
    ya = bg * (cw[0:1, :] * cu_m1 + cw[1:2, :] * cu + cw[2:3, :] * cu_p1)
    ya_ref[...] = ya.astype(_BF16)

    def head_norm(blk, hd, gain_ref):
        gm = gm_ref[0] if hd < DIFF_HEADS else gm_ref[1]
        ms = _dot((blk * blk).astype(_BF16), gm)
        return blk * lax.rsqrt(ms + EPS) * gain_ref[:, hd * LANES:(hd + 1) * LANES]

    qf = _dot(h, wq_ref[...])
    for hd in range(N_HEADS):
        qn = head_norm(qf[:, hd * LANES:(hd + 1) * LANES], hd, gq_ref)
        qt_ref[0, hd, 0] = qn.T.astype(_BF16)
    kf = _dot(h, wk_ref[...])
    for hd in range(N_HEADS):
        k_ref[0, hd] = head_norm(kf[:, hd * LANES:(hd + 1) * LANES], hd, gk_ref).astype(_BF16)
    vf = _dot(h, wv_ref[...])
    lane = lax.broadcasted_iota(jnp.int32, (TILE, LANES), 1)
    for hd in range(N_HEADS):
        vh = jnp.where(lane == HEAD_DIM, 1.0, vf[:, hd * LANES:(hd + 1) * LANES])
        vt_ref[0, hd, 0] = vh.T.astype(_BF16)


def _mixin(x2d, g, wc, wq, wk, wv, gq, gk, gm, cw, *, batch, seq):
    tps = seq // TILE
    halo = 8
    hpt = TILE // halo
    n_halo = x2d.shape[0] // halo
    pw = N_HEADS * LANES
    row = pl.BlockSpec((TILE, D_MODEL), lambda i: (i, 0))
    prev = pl.BlockSpec((halo, D_MODEL), lambda i: (jnp.maximum(i * hpt - 1, 0), 0))
    nxt = pl.BlockSpec((halo, D_MODEL), lambda i: (jnp.minimum((i + 1) * hpt, n_halo - 1), 0))
    head_rows = pl.BlockSpec((1, N_HEADS, TILE, LANES), lambda i: (i // tps, 0, i % tps, 0))
    head_cols = pl.BlockSpec((1, N_HEADS, 1, LANES, TILE), lambda i: (i // tps, 0, i % tps, 0, 0))
    return pl.pallas_call(
        functools.partial(_mixin_kernel, tiles_per_seq=tps),
        grid=(batch * tps,),
        in_specs=[row, prev, nxt, _resident((1, D_MODEL)), _resident((D_MODEL, 3 * CONV_CH)),
                  _resident((D_MODEL, pw)), _resident((D_MODEL, pw)), _resident((D_MODEL, pw)),
                  _resident((1, pw)), _resident((1, pw)), _resident((2, LANES, LANES)),
                  _resident((3, CONV_CH))],
        out_specs=[pl.BlockSpec((TILE, CONV_CH), lambda i: (i, 0)), head_cols, head_rows, head_cols],
        out_shape=[jax.ShapeDtypeStruct((batch * seq, CONV_CH), _BF16),
                   jax.ShapeDtypeStruct((batch, N_HEADS, tps, LANES, TILE), _BF16),
                   jax.ShapeDtypeStruct((batch, N_HEADS, seq, LANES), _BF16),
                   jax.ShapeDtypeStruct((batch, N_HEADS, tps, LANES, TILE), _BF16)],
        compiler_params=_params("parallel"),
        name="mixer_in",
    )(x2d, x2d, x2d, g, wc, wq, wk, wv, gq, gk, gm, cw)


def _strip_kernel(tab_ref, ref_ref, o_ref, *, band, head_off, dilated):
    hd = pl.program_id(0) + head_off
    off = pl.program_id(1) - band
    key = lax.broadcasted_iota(jnp.int32, (TILE, TILE), 0)
    qry = lax.broadcasted_iota(jnp.int32, (TILE, TILE), 1)
    rel = off * TILE + key - qry
    n = jnp.abs(rel)
    large = jnp.full(rel.shape, N_BUCKETS // 4, jnp.int32)
    for step in BUCKET_STEPS:
        large = large + jnp.where(n >= step, 1, 0)
    bucket = jnp.where(rel > 0, N_BUCKETS // 2, 0) + jnp.where(n < N_BUCKETS // 4, n, large)
    val = jnp.zeros(rel.shape, _F32)
    for b in range(N_BUCKETS):
        val = jnp.where(bucket == b, tab_ref[b, hd], val)
    if dilated:
        cnt = jnp.zeros(rel.shape, _F32)
        for window, dil in DIL_PAIRS:
            on_grid = jnp.where((n & (dil - 1)) == 0, 1.0, 0.0)
            cnt = cnt + jnp.where(n <= window // 2, on_grid, 0.0)
        val = val + jnp.log(jnp.maximum(cnt, 1.0))
    val = (val - ref_ref[0, hd]) * ref_ref[1, hd]
    if dilated:
        masked = jnp.logical_or(cnt == 0.0, pl.program_id(1) == 2 * band + 1)
        val = jnp.where(masked, NEG, val)
    o_ref[0, 0] = val


def _bias_strips(rel_bias, score_ref, *, n_heads, band, head_off, dilated):
    n_off = 2 * band + (2 if dilated else 1)
    return pl.pallas_call(
        functools.partial(_strip_kernel, band=band, head_off=head_off, dilated=dilated),
        grid=(n_heads, n_off),
        in_specs=[pl.BlockSpec(memory_space=pltpu.SMEM), pl.BlockSpec(memory_space=pltpu.SMEM)],
        out_specs=pl.BlockSpec((1, 1, TILE, TILE), lambda h, o: (h, o, 0, 0)),
        out_shape=jax.ShapeDtypeStruct((n_heads, n_off, TILE, TILE), _F32),
        compiler_params=_params("parallel", "parallel"),
        name="bias_strips_dil" if dilated else "bias_strips_diff",
    )(rel_bias, score_ref)


def _attn_kernel(qt_ref, k_ref, vt_ref, strip_ref, gsub_ref, lamv_ref, o_ref, m_ref, acc_ref, s_ref, *,
                 diff, band, n_tiles, lam_init, static_ref):
    i = pl.program_id(2)
    qt = qt_ref[0, 0, 0]
    feat = lax.broadcasted_iota(jnp.int32, (LANES, TILE), 0)
    if diff:
        zero = jnp.zeros_like(qt)
        qs = [jnp.where(feat < DIFF_QK_DIM, qt, zero), jnp.where(feat >= DIFF_QK_DIM, qt, zero)]
    else:
        qs = [qt]
    n_sub = len(qs)
    m_ref[...] = jnp.full(m_ref.shape, NEG, _F32)
    acc_ref[...] = jnp.zeros(acc_ref.shape, _F32)

    def body(j, carry):
        k = k_ref[0, 0, pl.ds(pl.multiple_of(j * TILE, TILE), TILE), :]
        vt = vt_ref[0, 0, j, :V_ROWS, :]
        bias = strip_ref[0, jnp.clip(j - i, -band, band) + band]
        for c in range(n_sub):
            s = _dot(k, qs[c]) + bias
            if static_ref:
                acc_ref[c, :V_ROWS, :] += _dot(vt, jnp.exp2(s).astype(_BF16))
                continue
            m_old = m_ref[c]
            m_new = jnp.maximum(m_old, jnp.max(s, axis=0, keepdims=True))
            alpha = jnp.exp(m_old - m_new)
            p = jnp.exp(s - m_new)
            acc_ref[c, :V_ROWS, :] = alpha * acc_ref[c, :V_ROWS, :] + _dot(vt, p.astype(_BF16))
            m_ref[c] = m_new
        return carry

    def scores(j, slot):
        k = k_ref[0, 0, pl.ds(pl.multiple_of(j * TILE, TILE), TILE), :]
        for c in range(n_sub):
            s_ref[slot, c] = _dot(k, qs[c])

    def consume(j, slot, strip_index):
        vt = vt_ref[0, 0, j, :V_ROWS, :]
        bias = strip_ref[0, strip_index]
        for c in range(n_sub):
            acc_ref[c, :V_ROWS, :] += _dot(vt, jnp.exp2(s_ref[slot, c] + bias).astype(_BF16))

    def group(j0, final):
        for u in range(KEY_TILES_PER_STEP):
            j = j0 + u
            if not (final and u == KEY_TILES_PER_STEP - 1):
                scores(j + 1, (u + 1) % 2)
            consume(j, u % 2, jnp.clip(j - i, -band, band) + band)

    def group_step(jj, carry):
        group(KEY_TILES_PER_STEP * jj, False)
        return carry

    if static_ref and diff and n_tiles % KEY_TILES_PER_STEP == 0:
        n_groups = n_tiles // KEY_TILES_PER_STEP
        scores(0, 0)
        lax.fori_loop(0, n_groups - 1, group_step, 0)
        group(KEY_TILES_PER_STEP * (n_groups - 1), True)
    elif static_ref and not diff:
        offsets = list(range(-band, band + 1))
        tiles = [jnp.clip(i + o, 0, n_tiles - 1) for o in offsets]
        scores(tiles[0], 0)
        for u, o in enumerate(offsets):
            if u + 1 < len(offsets):
                scores(tiles[u + 1], (u + 1) % 2)
            in_range = jnp.logical_and(i + o >= 0, i + o < n_tiles)
            consume(tiles[u], u % 2, jnp.where(in_range, o + band, 2 * band + 1))
    elif diff:
        lax.fori_loop(0, n_tiles, body, 0)
    else:
        lax.fori_loop(jnp.maximum(i - band, 0), jnp.minimum(i + band + 1, n_tiles), body, 0)

    outs = []
    for c in range(n_sub):
        a = acc_ref[c]
        outs.append(a / a[HEAD_DIM:HEAD_DIM + 1, :])
    if diff:
        lamv = lamv_ref[...]
        lam = (jnp.exp(jnp.sum(lamv[0:1] * lamv[1:2], axis=-1, keepdims=True))
               - jnp.exp(jnp.sum(lamv[2:3] * lamv[3:4], axis=-1, keepdims=True)) + lam_init)
        o = outs[0] - lam * outs[1]
        o = jnp.where(feat < HEAD_DIM, o, 0.0)
        ms = jnp.sum(o * o, axis=0, keepdims=True) * (1.0 / HEAD_DIM)
        y = o * lax.rsqrt(ms + EPS) * gsub_ref[...] * (1.0 - lam_init)
    else:
        y = jnp.where(feat < HEAD_DIM, outs[0], 0.0)
    o_ref[0, 0] = y.T.astype(_BF16)


def _attention(qt, k, vt, strips, gsub, lamv, *, diff, lam_init, static_ref):
    batch, _, seq, _ = k.shape
    n_tiles = seq // TILE
    n_heads = DIFF_HEADS if diff else DIL_HEADS
    head_off = 0 if diff else DIFF_HEADS
    band = DIFF_BAND if diff else DIL_BAND
    n_off = strips.shape[1]
    n_sub = 2 if diff else 1
    return pl.pallas_call(
        functools.partial(_attn_kernel, diff=diff, band=band, n_tiles=n_tiles, lam_init=lam_init,
                          static_ref=static_ref),
        grid=(batch, n_heads, n_tiles),
        in_specs=[pl.BlockSpec((1, 1, 1, LANES, TILE), lambda b, h, i: (b, h + head_off, i, 0, 0)),
                  pl.BlockSpec((1, 1, seq, LANES), lambda b, h, i: (b, h + head_off, 0, 0)),
                  pl.BlockSpec((1, 1, n_tiles, LANES, TILE), lambda b, h, i: (b, h + head_off, 0, 0, 0)),
                  pl.BlockSpec((1, n_off, TILE, TILE), lambda b, h, i: (h, 0, 0, 0)),
                  pl.BlockSpec((LANES, 1), lambda b, h, i: (0, 0)),
                  pl.BlockSpec((4, DIFF_QK_DIM), lambda b, h, i: (0, 0))],
        out_specs=pl.BlockSpec((1, 1, TILE, LANES), lambda b, h, i: (b, h, i, 0)),
        out_shape=jax.ShapeDtypeStruct((batch, n_heads, seq, LANES), _BF16),
        scratch_shapes=[pltpu.VMEM((n_sub, 1, TILE), _F32), pltpu.VMEM((n_sub, LANES, TILE), _F32),
                        pltpu.VMEM((2, n_sub, TILE, TILE), _F32)],
        compiler_params=_params("parallel", "parallel", "arbitrary"),
        name=("diff_attention" if diff else "dilated_attention") + ("" if static_ref else "_online"),
    )(qt, k, vt, strips, gsub, lamv)


def _mixout_kernel(x_ref, ya_ref, yb_ref, yc_ref, wa_ref, wh_ref, o_ref):
    def head_pair(ref, g):
        lower = ref[0, 2 * g].astype(_F32)
        upper = pltpu.roll(ref[0, 2 * g + 1].astype(_F32), HEAD_DIM, 1)
        return (lower + upper).astype(_BF16)

    acc = _dot(ya_ref[...], wa_ref[...])
    for g in range(DIFF_HEADS // 2):
        acc = acc + _dot(head_pair(yb_ref, g), wh_ref[g])
    for g in range(DIL_HEADS // 2):
        acc = acc + _dot(head_pair(yc_ref, g), wh_ref[DIFF_HEADS // 2 + g])
    o_ref[...] = x_ref[...] + acc


def _mixout(x2d, ya, yb, yc, wa, wh, *, batch, seq):
    tps = seq // TILE
    row = pl.BlockSpec((TILE, D_MODEL), lambda i: (i, 0))
    return pl.pallas_call(
        _mixout_kernel,
        grid=(batch * tps,),
        in_specs=[row, pl.BlockSpec((TILE, CONV_CH), lambda i: (i, 0)),
                  pl.BlockSpec((1, DIFF_HEADS, TILE, LANES), lambda i: (i // tps, 0, i % tps, 0)),
                  pl.BlockSpec((1, DIL_HEADS, TILE, LANES), lambda i: (i // tps, 0, i % tps, 0)),
                  _resident((CONV_CH, D_MODEL)), _resident((N_HEADS // 2, LANES, D_MODEL))],
        out_specs=row,
        out_shape=jax.ShapeDtypeStruct(x2d.shape, _F32),
        compiler_params=_params("parallel"),
        name="mixer_out",
    )(x2d, ya, yb, yc, wa, wh)


def _pad_heads(w, n_heads):
    w = w.reshape(w.shape[0], n_heads, HEAD_DIM)
    w = jnp.pad(w, ((0, 0), (0, 0), (0, LANES - HEAD_DIM)))
    return w.reshape(w.shape[0], n_heads * LANES)


def _head_gain(g_diff, g_dil, diff_scale, dil_scale):
    zeros = jnp.zeros((LANES - HEAD_DIM,), _F32)
    diff = jnp.concatenate([g_diff, g_diff, zeros]) * diff_scale
    dil = jnp.concatenate([g_dil, zeros]) * dil_scale
    return jnp.concatenate([jnp.tile(diff, DIFF_HEADS), jnp.tile(dil, DIL_HEADS)])[None, :]


def _group_mean_mats():
    r = jnp.arange(LANES)
    live = (r[:, None] < HEAD_DIM) & (r[None, :] < HEAD_DIM)
    same32 = (r[:, None] // DIFF_QK_DIM) == (r[None, :] // DIFF_QK_DIM)
    g_diff = jnp.where(live & same32, 1.0 / DIFF_QK_DIM, 0.0)
    g_dil = jnp.where(live, 1.0 / HEAD_DIM, 0.0)
    return jnp.stack([g_diff, g_dil]).astype(_BF16)


def _softmax_reference(diff_q_norm, diff_k_norm, dil_q_norm, dil_k_norm, rel_bias):
    def qk_bound(gq, gk, dim):
        return math.sqrt(dim) * jnp.max(jnp.abs(gq)) * jnp.max(jnp.abs(gk)) * NORM_SLACK

    qk = jnp.concatenate([jnp.full((DIFF_HEADS,), qk_bound(diff_q_norm, diff_k_norm, DIFF_QK_DIM)),
                          jnp.full((DIL_HEADS,), qk_bound(dil_q_norm, dil_k_norm, HEAD_DIM))])
    merged = jnp.concatenate([jnp.zeros((DIFF_HEADS,), _F32),
                              jnp.full((DIL_HEADS,), math.log(len(DIL_PAIRS)), _F32)])
    hi = jnp.max(rel_bias, axis=0) + merged
    lo = jnp.min(rel_bias, axis=0)
    upper = qk + hi
    ok = jnp.max(2.0 * qk + hi - lo) <= SAFE_LOGIT_SPAN
    return upper, ok


def kernel(x_prompt, x_sample, ffn1_norm, ffn1_w_gu, ffn1_w_down, mix_norm, w_in, conv_w, diff_q_norm, diff_k_norm,
           lambda_q1, lambda_k1, lambda_q2, lambda_k2, diff_sub_norm, dil_q_norm, dil_k_norm, w_out, ffn2_norm,
           ffn2_w_gu, ffn2_w_down, final_norm, rel_bias):
    n_prompt = x_prompt.shape[0]
    x = jnp.concatenate([x_prompt, x_sample], axis=0)
    batch, seq, _ = x.shape
    x = x.reshape(batch * seq, D_MODEL)

    upper, static_ok = _softmax_reference(diff_q_norm, diff_k_norm, dil_q_norm, dil_k_norm, rel_bias)
    logit_scale = jnp.where(static_ok, LOG2E, 1.0).astype(_F32)
    logit_shift = jnp.stack([jnp.where(static_ok, upper, 0.0), jnp.full((N_HEADS,), logit_scale)]).astype(_F32)
    strips_diff = _bias_strips(rel_bias, logit_shift, n_heads=DIFF_HEADS, band=DIFF_BAND, head_off=0,
                               dilated=False)
    strips_dil = _bias_strips(rel_bias, logit_shift, n_heads=DIL_HEADS, band=DIL_BAND, head_off=DIFF_HEADS,
                              dilated=True)
    gm = _group_mean_mats()
    zeros_gain = jnp.zeros((LANES, 1), _F32)
    c0 = 3 * CONV_CH
    c1 = c0 + 3 * DIFF_HEADS * HEAD_DIM

    for l in range(DEPTH):
        lam_init = 0.8 - 0.6 * math.exp(-0.3 * l)

        def ffn(xin, norm, w_gu, w_down, final):
            return _ffn(xin, norm[l][None, :], w_gu[l][:, :D_FF].astype(_BF16), w_gu[l][:, D_FF:].astype(_BF16),
                        w_down[l].astype(_BF16), final_norm[l][None, :], final_norm=final)

        x = ffn(x, ffn1_norm, ffn1_w_gu, ffn1_w_down, False)

        wl = w_in[l]
        dw = DIFF_HEADS * HEAD_DIM
        lw = DIL_HEADS * HEAD_DIM
        wq = jnp.concatenate([_pad_heads(wl[:, c0:c0 + dw], DIFF_HEADS),
                              _pad_heads(wl[:, c1:c1 + lw], DIL_HEADS)], axis=1).astype(_BF16)
        wk = jnp.concatenate([_pad_heads(wl[:, c0 + dw:c0 + 2 * dw], DIFF_HEADS),
                              _pad_heads(wl[:, c1 + lw:c1 + 2 * lw], DIL_HEADS)], axis=1).astype(_BF16)
        wv = jnp.concatenate([_pad_heads(wl[:, c0 + 2 * dw:c0 + 3 * dw], DIFF_HEADS),
                              _pad_heads(wl[:, c1 + 2 * lw:c1 + 3 * lw], DIL_HEADS)], axis=1).astype(_BF16)
        gq = _head_gain(diff_q_norm[l], dil_q_norm[l], DIFF_QK_DIM ** -0.5, HEAD_DIM ** -0.5) * logit_scale
        gk = _head_gain(diff_k_norm[l], dil_k_norm[l], 1.0, 1.0)
        ya, qt, k, vt = _mixin(x, mix_norm[l][None, :], wl[:, :c0].astype(_BF16), wq, wk, wv, gq, gk, gm, conv_w[l],
                              batch=batch, seq=seq)

        gsub = jnp.concatenate([diff_sub_norm[l], jnp.zeros((LANES - HEAD_DIM,), _F32)])[:, None]
        lamv = jnp.stack([lambda_q1[l], lambda_k1[l], lambda_q2[l], lambda_k2[l]]).astype(_F32)
        def attend(strips, gain, diff):
            branches = [functools.partial(_attention, diff=diff, lam_init=lam_init, static_ref=flag)
                        for flag in (True, False)]
            return lax.cond(static_ok, *branches, qt, k, vt, strips, gain, lamv)

        yb = attend(strips_diff, gsub, True)
        yc = attend(strips_dil, zeros_gain, False)

        wo = w_out[l]
        wh = wo[CONV_CH:].reshape(N_HEADS // 2, LANES, D_MODEL).astype(_BF16)
        x = _mixout(x, ya, yb, yc, wo[:CONV_CH].astype(_BF16), wh, batch=batch, seq=seq)

        x = ffn(x, ffn2_norm, ffn2_w_gu, ffn2_w_down, True)

    x = x.reshape(batch, seq, D_MODEL)
    return (x[:n_prompt], x[n_prompt:])
```
